```python
import math
import jax, jax.numpy as jnp
from jax import lax
import numpy as np

D_MODEL = 1024
BATCH = 4
SEQ = 8192
DEPTH = 1
DEC_BATCH = 128
DEC_SEQ = 8
PAST_LEN = 8192
PAGE_SIZE = 128

DIL_GROUPS = ((128, 1), (512, 4), (2048, 16))
N_GROUPS = 3
N_SLOTS = 8
HEAD_DIM = 64
ATT_W = N_GROUPS * N_SLOTS * HEAD_DIM
ATT_OUT = N_SLOTS * HEAD_DIM
ROPE_THETA = 10000.0
BAND_BLOCK = 128
D_CONV = D_MODEL
CONV_WIDTH = 3
PEER_HEADS = 8
PEER_N_KEYS = 128
PEER_N_EXPERTS = PEER_N_KEYS * PEER_N_KEYS
PEER_D_KEY = 256
PEER_TOPK = 16
PEER_CHUNK = 128
LN_EPS = 1e-5
DEEPNORM_ALPHA = (2.0 * DEPTH) ** 0.25
DEEPNORM_BETA = (8.0 * DEPTH) ** -0.25
N_IN = 3 * ATT_W + 3 * D_CONV + 2 * D_MODEL
NEG_INF = -1e30

kernel_name = "dilated_conv_peer_hybrid_step"


def layer_norm(x, g, b):
    xf = x.astype(jnp.float32)
    mu = jnp.mean(xf, axis=-1, keepdims=True)
    var = jnp.mean(jnp.square(xf - mu), axis=-1, keepdims=True)
    y = (xf - mu) * lax.rsqrt(var + LN_EPS) * g.astype(jnp.float32) + b.astype(jnp.float32)
    return y.astype(x.dtype)


def rope(x, pos):
    half = HEAD_DIM // 2
    inv_freq = ROPE_THETA ** (-jnp.arange(half, dtype=jnp.float32) / half)
    ang = pos.astype(jnp.float32)[:, None] * inv_freq[None, :]
    cos = jnp.cos(ang)[:, None, None, :]
    sin = jnp.sin(ang)[:, None, None, :]
    xf = x.astype(jnp.float32)
    x1, x2 = xf[..., :half], xf[..., half:]
    return jnp.concatenate([x1 * cos - x2 * sin, x2 * cos + x1 * sin], axis=-1).astype(x.dtype)


def mixer_projections(x, pos, w_in):
    n, s, _ = x.shape
    z = jnp.einsum('nsd,de->nse', x, w_in)
    bounds = [ATT_W, 2 * ATT_W, 3 * ATT_W, 3 * ATT_W + D_CONV, 3 * ATT_W + 2 * D_CONV,
              3 * ATT_W + 3 * D_CONV, 3 * ATT_W + 3 * D_CONV + D_MODEL]
    q, k, v, c_in, c_b, c_c, g_a, g_c = jnp.split(z, bounds, axis=-1)
    q = rope(q.reshape(n, s, N_GROUPS, N_SLOTS, HEAD_DIM), pos)
    k = rope(k.reshape(n, s, N_GROUPS, N_SLOTS, HEAD_DIM), pos)
    v = v.reshape(n, s, N_GROUPS, N_SLOTS, HEAD_DIM)
    u = c_c * c_in
    return q, k, v, u, c_b, g_a, g_c


def dilated_band_attention(q, k, v, dil, span):
    n, s, h, c = q.shape
    m = s // dil
    nb = -(-m // BAND_BLOCK)
    mp = nb * BAND_BLOCK

    def to_blocks(a):
        a = a.reshape(n, m, dil, h, c).transpose(0, 2, 1, 3, 4)
        a = jnp.pad(a, ((0, 0), (0, 0), (0, mp - m), (0, 0), (0, 0)))
        return a.reshape(n, dil, nb, BAND_BLOCK, h, c).astype(jnp.float32)

    qb, kb, vb = to_blocks(q), to_blocks(k), to_blocks(v)

    def with_prev(a):
        prev = jnp.pad(a, ((0, 0), (0, 0), (1, 0), (0, 0), (0, 0), (0, 0)))[:, :, :nb]
        return jnp.concatenate([prev, a], axis=3)

    kk, vv = with_prev(kb), with_prev(vb)
    scores = jnp.einsum('brnqhc,brnkhc->brnhqk', qb, kk) * (c ** -0.5)
    qi = jnp.arange(BAND_BLOCK)[:, None]
    ki = jnp.arange(2 * BAND_BLOCK)[None, :]
    dist = BAND_BLOCK + qi - ki
    in_band = (dist >= 0) & (dist <= span)
    has_prev = (jnp.arange(nb) > 0)[:, None, None] | (ki >= BAND_BLOCK)[None]
    mask = in_band[None] & has_prev
    scores = jnp.where(mask[:, None], scores, NEG_INF)
    mx = jnp.max(scores, axis=-1, keepdims=True)
    p = jnp.exp(scores - mx)
    den = jnp.sum(p, axis=-1, keepdims=True)
    out = jnp.einsum('brnhqk,brnkhc->brnqhc', p / den, vv)
    lse = (mx + jnp.log(den))[..., 0]
    out = out.reshape(n, dil, mp, h, c)[:, :, :m].transpose(0, 2, 1, 3, 4).reshape(n, s, h, c)
    lse = lse.transpose(0, 1, 2, 4, 3).reshape(n, dil, mp, h)[:, :, :m].transpose(0, 2, 1, 3).reshape(n, s, h)
    return out, lse


def dilated_cached_attention(q, k_all, v_all, n_prev, dil, span):
    t = q.shape[1]
    c = q.shape[-1]
    idx = n_prev + jnp.arange(t)[:, None] - dil * jnp.arange(span + 1)[None, :]
    valid = idx >= 0
    idx = jnp.maximum(idx, 0)
    kg = jnp.take(k_all, idx, axis=1).astype(jnp.float32)
    vg = jnp.take(v_all, idx, axis=1).astype(jnp.float32)
    scores = jnp.einsum('nthc,ntjhc->nthj', q.astype(jnp.float32), kg) * (c ** -0.5)
    scores = jnp.where(valid[None, :, None, :], scores, NEG_INF)
    mx = jnp.max(scores, axis=-1, keepdims=True)
    p = jnp.exp(scores - mx)
    den = jnp.sum(p, axis=-1, keepdims=True)
    out = jnp.einsum('nthj,ntjhc->nthc', p / den, vg)
    lse = (mx + jnp.log(den))[..., 0]
    return out, lse


def combine_groups(outs, lses, dtype):
    w = jax.nn.softmax(jnp.stack(lses, axis=0), axis=0)
    o = jnp.einsum('gnsh,gnshc->nshc', w, jnp.stack(outs, axis=0))
    n, s = o.shape[0], o.shape[1]
    return o.reshape(n, s, ATT_OUT).astype(dtype)


def short_conv(u_ext, conv_w):
    t = u_ext.shape[1] - (CONV_WIDTH - 1)
    z = u_ext[:, 0:t] * conv_w[0]
    for j in range(1, CONV_WIDTH):
        z = z + u_ext[:, j:j + t] * conv_w[j]
    return z


def peer_ffn(h, peer_w_q, peer_keys_1, peer_keys_2, peer_u, peer_v):
    n, s, d = h.shape
    tok = h.reshape(n * s, d)
    n_tok = n * s
    n_pad = -(-n_tok // PEER_CHUNK) * PEER_CHUNK
    tok = jnp.pad(tok, ((0, n_pad - n_tok), (0, 0)))
    half = PEER_D_KEY // 2
    k1 = peer_keys_1.astype(jnp.float32)
    k2 = peer_keys_2.astype(jnp.float32)

    def block(xc):
        qy = jnp.einsum('cd,de->ce', xc, peer_w_q).astype(jnp.float32)
        qy = qy.reshape(xc.shape[0], PEER_HEADS, PEER_D_KEY)
        s1 = jnp.einsum('chk,nk->chn', qy[..., :half], k1)
        s2 = jnp.einsum('chk,nk->chn', qy[..., half:], k2)
        v1, i1 = lax.top_k(s1, PEER_TOPK)
        v2, i2 = lax.top_k(s2, PEER_TOPK)
        cand = (v1[..., :, None] + v2[..., None, :]).reshape(xc.shape[0], PEER_HEADS, PEER_TOPK * PEER_TOPK)
        cid = (i1[..., :, None] * PEER_N_KEYS + i2[..., None, :]).reshape(xc.shape[0], PEER_HEADS, PEER_TOPK * PEER_TOPK)
        sc, pos = lax.top_k(cand, PEER_TOPK)
        eid = jnp.take_along_axis(cid, pos, axis=-1)
        g = jax.nn.softmax(sc, axis=-1)
        u = jnp.take(peer_u, eid, axis=0)
        act = jax.nn.gelu(jnp.einsum('cd,chkd->chk', xc, u).astype(jnp.float32), approximate=False)
        vv = jnp.take(peer_v, eid, axis=0)
        return jnp.einsum('chk,chkd->cd', (g * act).astype(xc.dtype), vv)

    out = lax.map(block, tok.reshape(n_pad // PEER_CHUNK, PEER_CHUNK, d))
    return out.reshape(n_pad, d)[:n_tok].reshape(n, s, d)


def finish_layer(x, y_att, z_conv, c_b, g_a, g_c, w_branch_attn, w_branch_conv, w_out,
                 ln1_g, ln1_b, peer_w_q, peer_keys_1, peer_keys_2, peer_u, peer_v, ln2_g, ln2_b):
    br_a = jnp.einsum('nse,ed->nsd', y_att, w_branch_attn)
    br_c = jnp.einsum('nse,ed->nsd', c_b * z_conv, w_branch_conv)
    merged = jax.nn.sigmoid(g_a) * br_a + jax.nn.sigmoid(g_c) * br_c
    mix = jnp.einsum('nsd,de->nse', merged, w_out)
    h = layer_norm(DEEPNORM_ALPHA * x + mix, ln1_g, ln1_b)
    f = peer_ffn(h, peer_w_q, peer_keys_1, peer_keys_2, peer_u, peer_v)
    return layer_norm(DEEPNORM_ALPHA * h + f, ln2_g, ln2_b)


def setup_inputs(seed: int = 0) -> dict:
    key = jax.random.key(seed)
    ks = jax.random.split(key, 24)

    def nrm(k, shape, scale):
        return jax.random.normal(k, shape, jnp.float32) * scale

    x_prompt = nrm(ks[0], (BATCH, SEQ, D_MODEL), 1.0)
    x_sample = nrm(ks[1], (DEC_BATCH, DEC_SEQ, D_MODEL), 1.0)
    cache_kv_w128 = nrm(ks[2], (DEPTH, DEC_BATCH, min(128, PAST_LEN), 2, N_SLOTS, HEAD_DIM), 1.0)
    cache_kv_w512 = nrm(ks[3], (DEPTH, DEC_BATCH, min(512, PAST_LEN), 2, N_SLOTS, HEAD_DIM), 1.0)
    cache_kv_w2048 = nrm(ks[4], (DEPTH, DEC_BATCH, min(2048, PAST_LEN), 2, N_SLOTS, HEAD_DIM), 1.0)
    state_conv = nrm(ks[5], (DEPTH, DEC_BATCH, CONV_WIDTH - 1, D_CONV), 1.0)
    w_qk = nrm(ks[6], (DEPTH, D_MODEL, 2 * ATT_W), D_MODEL ** -0.5)
    w_v = nrm(ks[7], (DEPTH, D_MODEL, ATT_W), D_MODEL ** -0.5 * DEEPNORM_BETA)
    w_rest = nrm(ks[8], (DEPTH, D_MODEL, 3 * D_CONV + 2 * D_MODEL), D_MODEL ** -0.5)
    w_in = jnp.concatenate([w_qk, w_v, w_rest], axis=-1)
    conv_w = nrm(ks[9], (DEPTH, CONV_WIDTH, D_CONV), CONV_WIDTH ** -0.5)
    w_branch_attn = nrm(ks[10], (DEPTH, ATT_OUT, D_MODEL), ATT_OUT ** -0.5)
    w_branch_conv = nrm(ks[11], (DEPTH, D_CONV, D_MODEL), D_CONV ** -0.5)
    w_out = nrm(ks[12], (DEPTH, D_MODEL, D_MODEL), D_MODEL ** -0.5 * DEEPNORM_BETA)
    ln1_g = 1.0 + nrm(ks[13], (DEPTH, D_MODEL), 0.05)
    ln1_b = nrm(ks[14], (DEPTH, D_MODEL), 0.02)
    peer_w_q = nrm(ks[15], (DEPTH, D_MODEL, PEER_HEADS * PEER_D_KEY), D_MODEL ** -0.5)
    peer_keys_1 = nrm(ks[16], (DEPTH, PEER_N_KEYS, PEER_D_KEY // 2), (PEER_D_KEY // 2) ** -0.5)
    peer_keys_2 = nrm(ks[17], (DEPTH, PEER_N_KEYS, PEER_D_KEY // 2), (PEER_D_KEY // 2) ** -0.5)
    peer_u = nrm(ks[18], (DEPTH, PEER_N_EXPERTS, D_MODEL), D_MODEL ** -0.5)
    peer_v = nrm(ks[19], (DEPTH, PEER_N_EXPERTS, D_MODEL), DEEPNORM_BETA)
    ln2_g = 1.0 + nrm(ks[20], (DEPTH, D_MODEL), 0.05)
    ln2_b = nrm(ks[21], (DEPTH, D_MODEL), 0.02)
    return {"x_prompt": x_prompt, "x_sample": x_sample,
            "cache_kv_w128": cache_kv_w128, "cache_kv_w512": cache_kv_w512,
            "cache_kv_w2048": cache_kv_w2048, "state_conv": state_conv,
            "w_in": w_in, "conv_w": conv_w, "w_branch_attn": w_branch_attn,
            "w_branch_conv": w_branch_conv, "w_out": w_out, "ln1_g": ln1_g, "ln1_b": ln1_b,
            "peer_w_q": peer_w_q, "peer_keys_1": peer_keys_1, "peer_keys_2": peer_keys_2,
            "peer_u": peer_u, "peer_v": peer_v, "ln2_g": ln2_g, "ln2_b": ln2_b}


def reference(x_prompt, x_sample, cache_kv_w128, cache_kv_w512, cache_kv_w2048, state_conv,
              w_in, conv_w, w_branch_attn, w_branch_conv, w_out, ln1_g, ln1_b,
              peer_w_q, peer_keys_1, peer_keys_2, peer_u, peer_v, ln2_g, ln2_b):
    caches = (cache_kv_w128, cache_kv_w512, cache_kv_w2048)
    seq = x_prompt.shape[1]
    dec_seq = x_sample.shape[1]
    pos_p = jnp.arange(seq)
    pos_s = PAST_LEN + jnp.arange(dec_seq)
    xp, xs = x_prompt, x_sample
    kv_p = [[], [], []]
    kv_s = [[], [], []]
    conv_p = []
    conv_s = []
    for layer in range(DEPTH):
        tail = (w_branch_attn[layer], w_branch_conv[layer], w_out[layer], ln1_g[layer], ln1_b[layer],
                peer_w_q[layer], peer_keys_1[layer], peer_keys_2[layer], peer_u[layer], peer_v[layer],
                ln2_g[layer], ln2_b[layer])

        q, k, v, u, c_b, g_a, g_c = mixer_projections(xp, pos_p, w_in[layer])
        outs, lses = [], []
        for gi, (win, dil) in enumerate(DIL_GROUPS):
            o, lse = dilated_band_attention(q[:, :, gi], k[:, :, gi], v[:, :, gi], dil, win // dil)
            outs.append(o)
            lses.append(lse)
            keep = min(win, seq)
            kv_p[gi].append(jnp.stack([k[:, seq - keep:, gi], v[:, seq - keep:, gi]], axis=2))
        y_att = combine_groups(outs, lses, xp.dtype)
        u_ext = jnp.pad(u, ((0, 0), (CONV_WIDTH - 1, 0), (0, 0)))
        z_conv = short_conv(u_ext, conv_w[layer])
        conv_p.append(u_ext[:, u_ext.shape[1] - (CONV_WIDTH - 1):])
        xp_next = finish_layer(xp, y_att, z_conv, c_b, g_a, g_c, *tail)

        q, k, v, u, c_b, g_a, g_c = mixer_projections(xs, pos_s, w_in[layer])
        outs, lses = [], []
        for gi, (win, dil) in enumerate(DIL_GROUPS):
            buf = caches[gi][layer]
            n_prev = buf.shape[1]
            k_all = jnp.concatenate([buf[:, :, 0], k[:, :, gi]], axis=1)
            v_all = jnp.concatenate([buf[:, :, 1], v[:, :, gi]], axis=1)
            o, lse = dilated_cached_attention(q[:, :, gi], k_all, v_all, n_prev, dil, win // dil)
            outs.append(o)
            lses.append(lse)
            n_all = k_all.shape[1]
            kv_s[gi].append(jnp.stack([k_all[:, n_all - n_prev:], v_all[:, n_all - n_prev:]], axis=2))
        y_att = combine_groups(outs, lses, xs.dtype)
        u_ext = jnp.concatenate([state_conv[layer].astype(u.dtype), u], axis=1)
        z_conv = short_conv(u_ext, conv_w[layer])
        conv_s.append(u_ext[:, u_ext.shape[1] - (CONV_WIDTH - 1):])
        xs_next = finish_layer(xs, y_att, z_conv, c_b, g_a, g_c, *tail)

        xp, xs = xp_next, xs_next

    return (xp, xs,
            jnp.stack(kv_p[0]), jnp.stack(kv_p[1]), jnp.stack(kv_p[2]), jnp.stack(conv_p),
            jnp.stack(kv_s[0]), jnp.stack(kv_s[1]), jnp.stack(kv_s[2]), jnp.stack(conv_s))
```

```python
import functools
import math

import jax
import jax.numpy as jnp
from jax import lax
from jax.experimental import pallas as pl
from jax.experimental.pallas import tpu as pltpu

F32 = jnp.float32
BF16 = jnp.bfloat16

D_MODEL = 1024
N_GROUPS = 3
N_SLOTS = 8
HEAD_DIM = 64
GROUP_W = N_SLOTS * HEAD_DIM
ATT_W = N_GROUPS * GROUP_W
DIL_GROUPS = ((128, 1), (512, 4), (2048, 16))
SPAN = 128
BAND_BLOCK = 128
ROPE_THETA = 10000.0
CONV_WIDTH = 3
PEER_HEADS = 8
PEER_N_KEYS = 128
PEER_D_KEY = 256
PEER_TOPK = 16
LN_EPS = 1e-5
NEG_INF = -1e30

COL_TILE = 512
_QK_TILE0, _V_TILE0, _CIN_TILE0, _CB_TILE0, _CC_TILE0, _GA_TILE0 = 0, 6, 9, 11, 13, 15

VMEM_LIMIT = 56 * 1024 * 1024


def _cparams(sem):
    return pltpu.CompilerParams(dimension_semantics=sem, vmem_limit_bytes=VMEM_LIMIT)


def _proj_rope_kernel(x_ref, w_ref, cos_ref, sin_ref, o_ref, *, n_q_tiles):
    j = pl.program_id(1)
    acc = jnp.dot(x_ref[...].astype(BF16), w_ref[...], preferred_element_type=F32)
    cos = cos_ref[...]
    sin = sin_ref[...]
    lane = lax.broadcasted_iota(jnp.int32, cos.shape, 1)
    first_half = (lane & (HEAD_DIM // 2)) == 0
    scale = jnp.where(j < n_q_tiles, HEAD_DIM ** -0.5, 1.0).astype(F32)
    for c in range(COL_TILE // 128):
        a = acc[:, c * 128:(c + 1) * 128]
        partner = jnp.where(first_half, pltpu.roll(a, 128 - HEAD_DIM // 2, 1), pltpu.roll(a, HEAD_DIM // 2, 1))
        o_ref[:, c * 128:(c + 1) * 128] = (a * cos + partner * sin) * scale


def _proj_kernel(x_ref, w_ref, o_ref):
    o_ref[...] = jnp.dot(x_ref[...].astype(BF16), w_ref[...], preferred_element_type=F32)


def _proj_product_kernel(x_ref, w1_ref, w2_ref, o_ref):
    xb = x_ref[...].astype(BF16)
    o_ref[...] = (jnp.dot(xb, w1_ref[...], preferred_element_type=F32)
                  * jnp.dot(xb, w2_ref[...], preferred_element_type=F32))


def _input_projections(x, w_bf, cos_t, sin_t):
    t = x.shape[0]
    tm = min(1024, t)
    nt = t // tm
    x_spec = pl.BlockSpec((tm, D_MODEL), lambda i, j: (i, 0))

    def w_spec(col_fn):
        return pl.BlockSpec((D_MODEL, COL_TILE), lambda i, j: (0, col_fn(j)))

    o_spec = pl.BlockSpec((tm, COL_TILE), lambda i, j: (i, j))
    sem = ("parallel", "arbitrary")

    n_qk = 2 * ATT_W // COL_TILE
    qk = pl.pallas_call(
        functools.partial(_proj_rope_kernel, n_q_tiles=ATT_W // COL_TILE),
        grid=(nt, n_qk),
        in_specs=[x_spec, w_spec(lambda j: j + _QK_TILE0),
                  pl.BlockSpec((tm, 128), lambda i, j: (i, 0)),
                  pl.BlockSpec((tm, 128), lambda i, j: (i, 0))],
        out_specs=o_spec,
        out_shape=jax.ShapeDtypeStruct((t, 2 * ATT_W), F32),
        compiler_params=_cparams(sem), name="proj_qk_rope",
    )(x, w_bf, cos_t, sin_t)

    v = pl.pallas_call(
        _proj_kernel, grid=(nt, ATT_W // COL_TILE),
        in_specs=[x_spec, w_spec(lambda j: j + _V_TILE0)],
        out_specs=o_spec, out_shape=jax.ShapeDtypeStruct((t, ATT_W), F32),
        compiler_params=_cparams(sem), name="proj_v",
    )(x, w_bf)

    u = pl.pallas_call(
        _proj_product_kernel, grid=(nt, D_MODEL // COL_TILE),
        in_specs=[x_spec, w_spec(lambda j: j + _CIN_TILE0), w_spec(lambda j: j + _CC_TILE0)],
        out_specs=o_spec, out_shape=jax.ShapeDtypeStruct((t, D_MODEL), F32),
        compiler_params=_cparams(sem), name="proj_conv_u",
    )(x, w_bf, w_bf)

    rest = pl.pallas_call(
        _proj_kernel, grid=(nt, 3 * D_MODEL // COL_TILE),
        in_specs=[x_spec, w_spec(lambda j: jnp.where(j < 2, j + _CB_TILE0, j - 2 + _GA_TILE0))],
        out_specs=o_spec, out_shape=jax.ShapeDtypeStruct((t, 3 * D_MODEL), F32),
        compiler_params=_cparams(sem), name="proj_cb_gates",
    )(x, w_bf)
    return qk, v, u, rest


def _band_attn_kernel(q_ref, kp_ref, kc_ref, vp_ref, vc_ref, o_ref, l_ref):
    mb = pl.program_id(2)
    qi = lax.broadcasted_iota(jnp.int32, (BAND_BLOCK, 2 * BAND_BLOCK), 0)
    ki = lax.broadcasted_iota(jnp.int32, (BAND_BLOCK, 2 * BAND_BLOCK), 1)
    dist = BAND_BLOCK + qi - ki
    prev_ok = jnp.where(mb > 0, 0, BAND_BLOCK)
    mask = (dist >= 0) & (dist <= SPAN) & (ki >= prev_ok)
    q = q_ref[...].astype(BF16)
    k = jnp.concatenate([kp_ref[...], kc_ref[...]], axis=0).astype(BF16)
    v = jnp.concatenate([vp_ref[...], vc_ref[...]], axis=0).astype(BF16)
    for h in range(N_SLOTS):
        sl = slice(h * HEAD_DIM, (h + 1) * HEAD_DIM)
        s = lax.dot_general(q[:, sl], k[:, sl], (((1,), (1,)), ((), ())), preferred_element_type=F32)
        s = jnp.where(mask, s, NEG_INF)
        mx = jnp.max(s, axis=-1, keepdims=True)
        p = jnp.exp(s - mx)
        den = jnp.sum(p, axis=-1, keepdims=True)
        o = jnp.dot(p.astype(BF16), v[:, sl], preferred_element_type=F32) / den
        o_ref[:, sl] = o
        l_ref[:, sl] = jnp.broadcast_to(mx + jnp.log(den), (BAND_BLOCK, HEAD_DIM))


def _prompt_attention(qk, v, n, s, gi, dil):
    m = s // dil
    nb = m // BAND_BLOCK
    n_qk_t = 2 * ATT_W // GROUP_W
    n_v_t = ATT_W // GROUP_W
    qk3 = qk.reshape(n, m, dil * 2 * ATT_W)
    v3 = v.reshape(n, m, dil * ATT_W)
    blk = (None, BAND_BLOCK, GROUP_W)

    def prev(b):
        return jnp.maximum(b - 1, 0)

    in_specs = [
        pl.BlockSpec(blk, lambda a, r, b: (a, b, r * n_qk_t + gi)),
        pl.BlockSpec(blk, lambda a, r, b: (a, prev(b), r * n_qk_t + N_GROUPS + gi)),
        pl.BlockSpec(blk, lambda a, r, b: (a, b, r * n_qk_t + N_GROUPS + gi)),
        pl.BlockSpec(blk, lambda a, r, b: (a, prev(b), r * n_v_t + gi)),
        pl.BlockSpec(blk, lambda a, r, b: (a, b, r * n_v_t + gi)),
    ]
    o_spec = pl.BlockSpec(blk, lambda a, r, b: (a, b, r))
    out, lse = pl.pallas_call(
        _band_attn_kernel, grid=(n, dil, nb),
        in_specs=in_specs, out_specs=[o_spec, o_spec],
        out_shape=[jax.ShapeDtypeStruct((n, m, dil * GROUP_W), F32)] * 2,
        compiler_params=_cparams(("parallel", "parallel", "arbitrary")), name=f"band_attn_d{dil}",
    )(qk3, qk3, qk3, v3, v3)
    return out.reshape(n * s, GROUP_W), lse.reshape(n * s, GROUP_W)


def _cached_attn_kernel(q_ref, kn_ref, vn_ref, kc_ref, vc_ref, ones_ref, o_ref, l_ref, *, dil, n_new, bb):
    r = pl.program_id(1)
    nq = max(n_new // dil, 1)
    ones = ones_ref[...]
    row_c = lax.broadcasted_iota(jnp.int32, (SPAN, GROUP_W), 0)
    row_n = lax.broadcasted_iota(jnp.int32, (n_new, GROUP_W), 0)

    def body(b, carry):
        kc = kc_ref[b]
        vc = vc_ref[b]
        kn = kn_ref[b]
        vn = vn_ref[b]
        for a in range(nq):
            t = r + dil * a
            q = q_ref[b, pl.ds(t, 1), :]
            sc = jnp.dot(kc * q, ones, preferred_element_type=F32)
            sn = jnp.dot(kn * q, ones, preferred_element_type=F32)
            sc = jnp.where(row_c >= a, sc, NEG_INF)
            ok_n = (row_n <= t) & (((t - row_n) & (dil - 1)) == 0)
            sn = jnp.where(ok_n, sn, NEG_INF)
            mx = jnp.maximum(jnp.max(sc, axis=0, keepdims=True), jnp.max(sn, axis=0, keepdims=True))
            pc = jnp.exp(sc - mx)
            pn = jnp.exp(sn - mx)
            den = jnp.sum(pc, axis=0, keepdims=True) + jnp.sum(pn, axis=0, keepdims=True)
            num = jnp.sum(pc * vc, axis=0, keepdims=True) + jnp.sum(pn * vn, axis=0, keepdims=True)
            o_ref[b, pl.ds(t, 1), :] = num / den
            l_ref[b, pl.ds(t, 1), :] = mx + jnp.log(den)
        return carry

    lax.fori_loop(0, bb, body, 0)


def _sample_attention(qk, v, cache, ones_bd, nb_, n_new, gi, win, dil):
    n_res = min(dil, n_new)
    bb = 8 if nb_ % 8 == 0 else 1
    qk3 = qk.reshape(nb_, n_new, 2 * ATT_W)
    v3 = v.reshape(nb_, n_new, ATT_W)
    cache3 = cache.reshape(nb_, win // dil, dil * 2 * GROUP_W)
    new_blk = (bb, n_new, GROUP_W)
    c_blk = (bb, win // dil, GROUP_W)
    in_specs = [
        pl.BlockSpec(new_blk, lambda i, r: (i, 0, gi)),
        pl.BlockSpec(new_blk, lambda i, r: (i, 0, N_GROUPS + gi)),
        pl.BlockSpec(new_blk, lambda i, r: (i, 0, gi)),
        pl.BlockSpec(c_blk, lambda i, r: (i, 0, 2 * r)),
        pl.BlockSpec(c_blk, lambda i, r: (i, 0, 2 * r + 1)),
        pl.BlockSpec((GROUP_W, GROUP_W), lambda i, r: (0, 0)),
    ]
    o_spec = pl.BlockSpec(new_blk, lambda i, r: (i, 0, 0))
    out, lse = pl.pallas_call(
        functools.partial(_cached_attn_kernel, dil=dil, n_new=n_new, bb=bb),
        grid=(nb_ // bb, n_res),
        in_specs=in_specs, out_specs=[o_spec, o_spec],
        out_shape=[jax.ShapeDtypeStruct((nb_, n_new, GROUP_W), F32)] * 2,
        compiler_params=_cparams(("parallel", "arbitrary")), name=f"cached_attn_d{dil}",
    )(qk3, qk3, v3, cache3, cache3, ones_bd)
    return out.reshape(nb_ * n_new, GROUP_W), lse.reshape(nb_ * n_new, GROUP_W)


def _layer_norm(x, g, b):
    mu = jnp.mean(x, axis=-1, keepdims=True)
    xc = x - mu
    var = jnp.mean(xc * xc, axis=-1, keepdims=True)
    return xc * lax.rsqrt(var + LN_EPS) * g + b


def _mix_kernel(x_ref, o0_ref, o1_ref, o2_ref, l0_ref, l1_ref, l2_ref, u_ref, h1_ref, h2_ref,
                cb_ref, ga_ref, gc_ref, cw_ref, wpa_ref, wpc_ref, wo_ref, g_ref, b_ref, out_ref,
                *, seg_len, alpha):
    tm = x_ref.shape[0]
    l0, l1, l2 = l0_ref[...], l1_ref[...], l2_ref[...]
    lm = jnp.maximum(jnp.maximum(l0, l1), l2)
    w0, w1, w2 = jnp.exp(l0 - lm), jnp.exp(l1 - lm), jnp.exp(l2 - lm)
    y_att = (w0 * o0_ref[...] + w1 * o1_ref[...] + w2 * o2_ref[...]) / (w0 + w1 + w2)
    u = u_ref[...]
    row = lax.broadcasted_iota(jnp.int32, u.shape, 0)
    if seg_len >= tm:
        keep = jnp.where((pl.program_id(0) * tm) % seg_len == 0, 0.0, 1.0).astype(F32)
        hrow1 = h1_ref[7:8, :] * keep
        hrow2 = h1_ref[6:7, :] * keep
        prev1 = jnp.where(row == 0, hrow1, pltpu.roll(u, 1, 0))
        prev2 = jnp.where(row == 0, hrow2, jnp.where(row == 1, hrow1, pltpu.roll(u, 2, 0)))
    else:
        rs = row & (seg_len - 1)
        prev1 = jnp.where(rs == 0, h1_ref[...], pltpu.roll(u, 1, 0))
        prev2 = jnp.where(rs < 2, h2_ref[...], pltpu.roll(u, 2, 0))
    cw = cw_ref[...]
    z_conv = prev2 * cw[0:1, :] + prev1 * cw[1:2, :] + u * cw[2:3, :]
    cv = cb_ref[...] * z_conv
    br_a = jnp.dot(y_att.astype(BF16), wpa_ref[...], preferred_element_type=F32)
    br_c = jnp.dot(cv.astype(BF16), wpc_ref[...], preferred_element_type=F32)
    merged = jax.nn.sigmoid(ga_ref[...]) * br_a + jax.nn.sigmoid(gc_ref[...]) * br_c
    mix = jnp.dot(merged.astype(BF16), wo_ref[...], preferred_element_type=F32)
    out_ref[...] = _layer_norm(alpha * x_ref[...] + mix, g_ref[...], b_ref[...])


def _mixer_finish(x, outs, lses, u, rest, hist1, hist2, seg_len, conv_w, wpa, wpc, wo, ln_g, ln_b, alpha):
    t = x.shape[0]
    tm = min(512, t)
    nt = t // tm
    row_spec = pl.BlockSpec((tm, D_MODEL), lambda i: (i, 0))
    att_spec = pl.BlockSpec((tm, GROUP_W), lambda i: (i, 0))
    if seg_len >= tm:
        hb = tm // 8
        h1_spec = pl.BlockSpec((8, D_MODEL), lambda i: (jnp.maximum(i * hb - 1, 0), 0))
        h2_spec = h1_spec
    else:
        h1_spec = h2_spec = row_spec

    def full(shape):
        return pl.BlockSpec(shape, lambda i: (0,) * len(shape))

    in_specs = ([row_spec] + [att_spec] * 6 + [row_spec, h1_spec, h2_spec]
                + [pl.BlockSpec((tm, D_MODEL), lambda i, c=c: (i, c)) for c in range(3)]
                + [full((CONV_WIDTH, D_MODEL)), full((GROUP_W, D_MODEL)), full((D_MODEL, D_MODEL)),
                   full((D_MODEL, D_MODEL)), full((1, D_MODEL)), full((1, D_MODEL))])
    return pl.pallas_call(
        functools.partial(_mix_kernel, seg_len=seg_len, alpha=alpha),
        grid=(nt,), in_specs=in_specs, out_specs=row_spec,
        out_shape=jax.ShapeDtypeStruct((t, D_MODEL), F32),
        compiler_params=_cparams(("parallel",)), name="mixer_finish",
    )(x, *outs, *lses, u, hist1, hist2, rest, rest, rest, conv_w, wpa, wpc, wo, ln_g, ln_b)


def _ranked_top16(s):
    n_keys = s.shape[0]
    rowf = lax.broadcasted_iota(jnp.int32, s.shape, 0).astype(F32)
    rank = jnp.full(s.shape, float(PEER_N_KEYS - 1), F32)
    tops = []
    for k in range(PEER_TOPK):
        m = jnp.max(s, axis=0, keepdims=True)
        pos = jnp.min(jnp.where(s == m, rowf, float(n_keys)), axis=0, keepdims=True)
        hit = rowf == pos
        s = jnp.where(hit, -jnp.inf, s)
        rank = jnp.where(hit, float(k), rank)
        tops.append(m)
    return rank, tops


def _rows_to_tile(rows):
    tmw = rows[0].shape[1]
    ri = lax.broadcasted_iota(jnp.int32, (8, tmw), 0)
    out = jnp.zeros((8, tmw), F32)
    for a, rw in enumerate(rows):
        out = jnp.where(ri == a, rw, out)
    return out


def _select_counts(t1, t2):
    tmw = t1[0].shape[1]
    ri = lax.broadcasted_iota(jnp.int32, (8, tmw), 0)
    rif = ri.astype(F32)
    ninf = -jnp.inf
    v1lo, v1hi = _rows_to_tile(t1[:8]), _rows_to_tile(t1[8:])
    v2lo, v2hi = _rows_to_tile(t2[:8]), _rows_to_tile(t2[8:])
    all_rows = ri >= 0
    tiles = [
        (t1[0] + v2lo, rif, all_rows),
        (t1[0] + v2hi, 8.0 + rif, all_rows),
        (t1[1] + v2lo, 16.0 + rif, all_rows),
        (v1hi + t2[0], (8.0 + rif) * 16.0, all_rows),
        (v1lo + t2[0], rif * 16.0, ri >= 2),
        (v1lo + t2[1], rif * 16.0 + 1.0, ri >= 2),
        (t1[2] + v2lo, 32.0 + rif, (ri >= 2) & (ri <= 4)),
        (t1[3] + v2lo, 48.0 + rif, (ri >= 2) & (ri <= 3)),
        (t1[4] + v2lo, 64.0 + rif, ri == 2),
    ]
    vals = [jnp.where(ok, sm, ninf) for sm, _, ok in tiles]
    poss = [jnp.where(ok, ps, -1.0) for _, ps, ok in tiles]
    hits = [jnp.zeros((8, tmw), F32) for _ in vals]
    top = t1[0] + t2[0]
    z = jnp.zeros((1, tmw), F32)
    big = 1e9
    for _ in range(PEER_TOPK):
        m8 = vals[0]
        for vv in vals[1:]:
            m8 = jnp.maximum(m8, vv)
        m = jnp.max(m8, axis=0, keepdims=True)
        p8 = jnp.where(vals[0] == m, poss[0], big)
        for vv, pp in zip(vals[1:], poss[1:]):
            p8 = jnp.minimum(p8, jnp.where(vv == m, pp, big))
        pm = jnp.min(p8, axis=0, keepdims=True)
        for c in range(len(vals)):
            hit = poss[c] == pm
            vals[c] = jnp.where(hit, ninf, vals[c])
            hits[c] = jnp.where(hit, 1.0, hits[c])
        z = z + jnp.exp(m - top)

    def colsum(x):
        return jnp.sum(x, axis=0, keepdims=True)

    lo = hits[4] + hits[5]
    counts = [colsum(hits[0]) + colsum(hits[1]), colsum(hits[2]),
              lo[2:3] + colsum(hits[6]), lo[3:4] + colsum(hits[7]), lo[4:5] + colsum(hits[8]),
              lo[5:6], lo[6:7], lo[7:8]]
    counts += [hits[3][a:a + 1] for a in range(8)]
    return counts, z


def _peer_kernel(h_ref, wq_ref, k1_ref, k2_ref, u_ref, vt_ref, g_ref, b_ref, out_ref,
                 ht_ref, c1_ref, e1_ref, r2_ref, e2_ref, a_ref, wt_ref, ft_ref, *, alpha, eb):
    e = pl.program_id(1)
    n_e = pl.num_programs(1)
    half = PEER_D_KEY // 2
    rows_per_step = eb // PEER_N_KEYS

    @pl.when(e == 0)
    def _select():
        ht = h_ref[...].T.astype(BF16)
        ht_ref[...] = ht
        ft_ref[...] = jnp.zeros(ft_ref.shape, F32)

        def head(hh, carry):
            off = pl.multiple_of(hh * PEER_D_KEY, PEER_D_KEY)
            qy = jnp.dot(wq_ref[pl.ds(off, PEER_D_KEY), :], ht_ref[...], preferred_element_type=F32)
            s1 = jnp.dot(k1_ref[...], qy[:half].astype(BF16), preferred_element_type=F32)
            s2 = jnp.dot(k2_ref[...], qy[half:].astype(BF16), preferred_element_type=F32)
            r1, t1 = _ranked_top16(s1)
            r2, t2 = _ranked_top16(s2)
            counts, z = _select_counts(t1, t2)
            c1 = jnp.zeros(s1.shape, F32)
            for a in range(PEER_TOPK):
                c1 = jnp.where(r1 == float(a), counts[a], c1)
            c1_ref[hh] = c1
            e1_ref[hh] = jnp.exp(s1 - t1[0]) / z
            r2_ref[hh] = r2
            e2_ref[hh] = jnp.exp(s2 - t2[0])
            return carry

        lax.fori_loop(0, PEER_HEADS, head, 0)

    a_ref[...] = jnp.dot(u_ref[...], ht_ref[...], preferred_element_type=F32)

    def expert_row(il, carry):
        i = e * rows_per_step + il
        r0 = pl.multiple_of(il * PEER_N_KEYS, PEER_N_KEYS)
        pre = a_ref[pl.ds(r0, PEER_N_KEYS), :]
        act = 0.5 * pre * (1.0 + lax.erf(pre * math.sqrt(0.5)))
        gate = jnp.zeros(pre.shape, F32)
        for hh in range(PEER_HEADS):
            cnt = c1_ref[hh, pl.ds(i, 1), :]
            wgt = e1_ref[hh, pl.ds(i, 1), :]
            gate = gate + jnp.where(r2_ref[hh] < cnt, wgt * e2_ref[hh], 0.0)
        wt_ref[pl.ds(r0, PEER_N_KEYS), :] = (gate * act).astype(BF16)
        return carry

    lax.fori_loop(0, rows_per_step, expert_row, 0)
    ft_ref[...] += jnp.dot(vt_ref[...], wt_ref[...], preferred_element_type=F32)

    @pl.when(e == n_e - 1)
    def _finish():
        f = ft_ref[...].T
        out_ref[...] = _layer_norm(alpha * h_ref[...] + f, g_ref[...], b_ref[...])


def _peer(h, wq_t, k1, k2, u_bf, vt_bf, ln_g, ln_b, alpha):
    t = h.shape[0]
    tm = min(512, t)
    eb = 1024
    n_exp = u_bf.shape[0]
    row_spec = pl.BlockSpec((tm, D_MODEL), lambda i, e: (i, 0))

    def full(shape):
        return pl.BlockSpec(shape, lambda i, e: (0,) * len(shape))

    meta = pltpu.VMEM((PEER_HEADS, PEER_N_KEYS, tm), F32)
    return pl.pallas_call(
        functools.partial(_peer_kernel, alpha=alpha, eb=eb),
        grid=(t // tm, n_exp // eb),
        in_specs=[row_spec, full(wq_t.shape), full(k1.shape), full(k2.shape),
                  pl.BlockSpec((eb, D_MODEL), lambda i, e: (e, 0)),
                  pl.BlockSpec((D_MODEL, eb), lambda i, e: (0, e)),
                  full((1, D_MODEL)), full((1, D_MODEL))],
        out_specs=row_spec,
        out_shape=jax.ShapeDtypeStruct((t, D_MODEL), F32),
        scratch_shapes=[pltpu.VMEM((D_MODEL, tm), BF16), meta, meta, meta, meta,
                        pltpu.VMEM((eb, tm), F32), pltpu.VMEM((eb, tm), BF16), pltpu.VMEM((D_MODEL, tm), F32)],
        compiler_params=_cparams(("parallel", "arbitrary")), name="peer",
    )(h, wq_t, k1, k2, u_bf, vt_bf, ln_g, ln_b)


def _rope_tables(pos):
    half = HEAD_DIM // 2
    inv_freq = ROPE_THETA ** (-jnp.arange(half, dtype=F32) / half)
    ang = pos.astype(F32)[:, None] * inv_freq[None, :]
    cos, sin = jnp.cos(ang), jnp.sin(ang)
    cos_t = jnp.concatenate([cos, cos, cos, cos], axis=-1)
    sin_t = jnp.concatenate([-sin, sin, -sin, sin], axis=-1)
    return cos_t, sin_t


def kernel(x_prompt, x_sample, cache_kv_w128, cache_kv_w512, cache_kv_w2048, state_conv, w_in, conv_w, w_branch_attn, w_branch_conv, w_out, ln1_g, ln1_b, peer_w_q, peer_keys_1, peer_keys_2, peer_u, peer_v, ln2_g, ln2_b):
    depth = w_in.shape[0]
    assert depth == 1, "single-layer trunk"
    n, s, d = x_prompt.shape
    nb_, n_new, _ = x_sample.shape
    past_len = 8192
    alpha = (2.0 * depth) ** 0.25
    caches = (cache_kv_w128, cache_kv_w512, cache_kv_w2048)
    for (win, dil), c in zip(DIL_GROUPS, caches):
        assert c.shape[2] == win and s % (dil * BAND_BLOCK) == 0 and win // dil == SPAN

    lyr = 0
    w_bf = w_in[lyr].astype(BF16)
    wpa = w_branch_attn[lyr].astype(BF16)
    wpc = w_branch_conv[lyr].astype(BF16)
    wo = w_out[lyr].astype(BF16)
    wq_t = peer_w_q[lyr].T.astype(BF16)
    k1 = peer_keys_1[lyr].astype(BF16)
    k2 = peer_keys_2[lyr].astype(BF16)
    u_bf = peer_u[lyr].astype(BF16)
    vt_bf = peer_v[lyr].T.astype(BF16)
    g1, b1 = ln1_g[lyr][None, :], ln1_b[lyr][None, :]
    g2, b2 = ln2_g[lyr][None, :], ln2_b[lyr][None, :]
    cw = conv_w[lyr]
    head_of_lane = jnp.arange(GROUP_W) // HEAD_DIM
    ones_bd = (head_of_lane[:, None] == head_of_lane[None, :]).astype(F32)

    xp = x_prompt.reshape(n * s, d)
    cos_p, sin_p = _rope_tables(jnp.tile(jnp.arange(s), n))
    qk_p, v_p, u_p, rest_p = _input_projections(xp, w_bf, cos_p, sin_p)
    outs, lses = [], []
    for gi, (win, dil) in enumerate(DIL_GROUPS):
        o, l = _prompt_attention(qk_p, v_p, n, s, gi, dil)
        outs.append(o)
        lses.append(l)
    h_p = _mixer_finish(xp, outs, lses, u_p, rest_p, u_p, u_p, s, cw, wpa, wpc, wo, g1, b1, alpha)
    y_p = _peer(h_p, wq_t, k1, k2, u_bf, vt_bf, g2, b2, alpha).reshape(n, s, d)

    qk_p4 = qk_p.reshape(n, s, 2, N_GROUPS, N_SLOTS, HEAD_DIM)
    v_p4 = v_p.reshape(n, s, N_GROUPS, N_SLOTS, HEAD_DIM)
    kv_p = []
    for gi, (win, dil) in enumerate(DIL_GROUPS):
        keep = min(win, s)
        kv_p.append(jnp.stack([qk_p4[:, s - keep:, 1, gi], v_p4[:, s - keep:, gi]], axis=2)[None])
    conv_state_p = u_p.reshape(n, s, d)[:, s - (CONV_WIDTH - 1):][None]

    xs = x_sample.reshape(nb_ * n_new, d)
    cos_s, sin_s = _rope_tables(jnp.tile(past_len + jnp.arange(n_new), nb_))
    qk_s, v_s, u_s, rest_s = _input_projections(xs, w_bf, cos_s, sin_s)
    outs, lses = [], []
    for gi, (win, dil) in enumerate(DIL_GROUPS):
        o, l = _sample_attention(qk_s, v_s, caches[gi][lyr], ones_bd, nb_, n_new, gi, win, dil)
        outs.append(o)
        lses.append(l)
    st = state_conv[lyr]
    zeros6 = jnp.zeros((nb_, n_new - 1, d), F32)
    hist1 = jnp.concatenate([st[:, 1:2], zeros6], axis=1).reshape(nb_ * n_new, d)
    hist2 = jnp.concatenate([st, zeros6[:, 1:]], axis=1).reshape(nb_ * n_new, d)
    h_s = _mixer_finish(xs, outs, lses, u_s, rest_s, hist1, hist2, n_new, cw, wpa, wpc, wo, g1, b1, alpha)
    y_s = _peer(h_s, wq_t, k1, k2, u_bf, vt_bf, g2, b2, alpha).reshape(nb_, n_new, d)

    qk_s4 = qk_s.reshape(nb_, n_new, 2, N_GROUPS, N_SLOTS, HEAD_DIM)
    v_s4 = v_s.reshape(nb_, n_new, N_GROUPS, N_SLOTS, HEAD_DIM)
    kv_s = []
    for gi, (win, dil) in enumerate(DIL_GROUPS):
        new_rows = jnp.stack([qk_s4[:, :, 1, gi], v_s4[:, :, gi]], axis=2)
        kv_s.append(jnp.concatenate([caches[gi][lyr][:, n_new:], new_rows], axis=1)[None])
    u_ext_s = jnp.concatenate([st, u_s.reshape(nb_, n_new, d)], axis=1)
    conv_state_s = u_ext_s[:, u_ext_s.shape[1] - (CONV_WIDTH - 1):][None]

    return (y_p, y_s, kv_p[0], kv_p[1], kv_p[2], conv_state_p,
            kv_s[0], kv_s[1], kv_s[2], conv_state_s)
```

```python
import functools
import math

import numpy as np
import jax
import jax.numpy as jnp
from jax import lax
from jax.experimental import pallas as pl
from jax.experimental.pallas import tpu as pltpu

F32 = jnp.float32
BF16 = jnp.bfloat16

LANES = 128
D_MODEL = 1024
N_GROUPS = 3
N_SLOTS = 8
HEAD_DIM = 64
GROUP_W = N_SLOTS * HEAD_DIM
GROUP_SLABS = GROUP_W // LANES
ATT_W = N_GROUPS * GROUP_W
DIL_GROUPS = ((128, 1), (512, 4), (2048, 16))
SPAN = 128
BAND_BLOCK = 128
ROPE_THETA = 10000.0
CONV_WIDTH = 3
PEER_HEADS = 8
PEER_N_KEYS = 128
PEER_D_KEY = 256
PEER_TOPK = 16
LN_EPS = 1e-5
NEG_INF = -1e30
TOKEN_CHUNK = 256
RESIDUE_UNROLL = 4

COL_TILE = 512
_QK_TILE0, _V_TILE0, _CIN_TILE0, _CB_TILE0, _CC_TILE0, _GA_TILE0 = 0, 6, 9, 11, 13, 15

VMEM_LIMIT = 56 * 1024 * 1024


def _cparams(sem):
    return pltpu.CompilerParams(dimension_semantics=sem, vmem_limit_bytes=VMEM_LIMIT)


def _proj_rope_kernel(x_ref, w_ref, cos_ref, sin_ref, o_ref, *, n_q_tiles):
    j = pl.program_id(1)
    acc = jnp.dot(x_ref[...].astype(BF16), w_ref[...], preferred_element_type=F32)
    cos = cos_ref[...]
    sin = sin_ref[...]
    lane = lax.broadcasted_iota(jnp.int32, cos.shape, 1)
    first_half = (lane & (HEAD_DIM // 2)) == 0
    scale = jnp.where(j < n_q_tiles, HEAD_DIM ** -0.5, 1.0).astype(F32)
    for c in range(COL_TILE // LANES):
        a = acc[:, c * LANES:(c + 1) * LANES]
        partner = jnp.where(first_half, pltpu.roll(a, LANES - HEAD_DIM // 2, 1), pltpu.roll(a, HEAD_DIM // 2, 1))
        o_ref[c] = (a * cos + partner * sin) * scale


def _proj_slab_kernel(x_ref, w_ref, o_ref):
    acc = jnp.dot(x_ref[...].astype(BF16), w_ref[...], preferred_element_type=F32)
    for c in range(COL_TILE // LANES):
        o_ref[c] = acc[:, c * LANES:(c + 1) * LANES]


def _proj_kernel(x_ref, w_ref, o_ref):
    o_ref[...] = jnp.dot(x_ref[...].astype(BF16), w_ref[...], preferred_element_type=F32)


def _proj_product_kernel(x_ref, w1_ref, w2_ref, o_ref):
    xb = x_ref[...].astype(BF16)
    o_ref[...] = (jnp.dot(xb, w1_ref[...], preferred_element_type=F32)
                  * jnp.dot(xb, w2_ref[...], preferred_element_type=F32))


def _input_projections(x, w_bf, cos_t, sin_t):
    t = x.shape[0]
    tm = min(1024, t)
    nt = t // tm
    n_pos_tiles = cos_t.shape[0] // tm
    x_spec = pl.BlockSpec((tm, D_MODEL), lambda i, j: (i, 0))

    def w_spec(col_fn):
        return pl.BlockSpec((D_MODEL, COL_TILE), lambda i, j: (0, col_fn(j)))

    o_spec = pl.BlockSpec((tm, COL_TILE), lambda i, j: (i, j))
    slab_spec = pl.BlockSpec((COL_TILE // LANES, tm, LANES), lambda i, j: (j, i, 0))
    pos_spec = pl.BlockSpec((tm, LANES), lambda i, j: (i % n_pos_tiles, 0))
    sem = ("parallel", "arbitrary")

    n_qk = 2 * ATT_W // COL_TILE
    qk = pl.pallas_call(
        functools.partial(_proj_rope_kernel, n_q_tiles=ATT_W // COL_TILE),
        grid=(nt, n_qk),
        in_specs=[x_spec, w_spec(lambda j: j + _QK_TILE0), pos_spec, pos_spec],
        out_specs=slab_spec,
        out_shape=jax.ShapeDtypeStruct((2 * ATT_W // LANES, t, LANES), F32),
        compiler_params=_cparams(sem), name="proj_qk_rope",
    )(x, w_bf, cos_t, sin_t)

    v = pl.pallas_call(
        _proj_slab_kernel, grid=(nt, ATT_W // COL_TILE),
        in_specs=[x_spec, w_spec(lambda j: j + _V_TILE0)],
        out_specs=slab_spec, out_shape=jax.ShapeDtypeStruct((ATT_W // LANES, t, LANES), F32),
        compiler_params=_cparams(sem), name="proj_v",
    )(x, w_bf)

    u = pl.pallas_call(
        _proj_product_kernel, grid=(nt, D_MODEL // COL_TILE),
        in_specs=[x_spec, w_spec(lambda j: j + _CIN_TILE0), w_spec(lambda j: j + _CC_TILE0)],
        out_specs=o_spec, out_shape=jax.ShapeDtypeStruct((t, D_MODEL), F32),
        compiler_params=_cparams(sem), name="proj_conv_u",
    )(x, w_bf, w_bf)

    rest = pl.pallas_call(
        _proj_kernel, grid=(nt, 3 * D_MODEL // COL_TILE),
        in_specs=[x_spec, w_spec(lambda j: jnp.where(j < 2, j + _CB_TILE0, j - 2 + _GA_TILE0))],
        out_specs=o_spec, out_shape=jax.ShapeDtypeStruct((t, 3 * D_MODEL), F32),
        compiler_params=_cparams(sem), name="proj_cb_gates",
    )(x, w_bf)
    return qk, v, u, rest


def _band_attn_kernel(q_ref, kp_ref, kc_ref, vp_ref, vc_ref, o_ref, l_ref, *, dil, nbb):
    b = pl.program_id(2)
    qi = lax.broadcasted_iota(jnp.int32, (BAND_BLOCK, 2 * BAND_BLOCK), 0)
    ki = lax.broadcasted_iota(jnp.int32, (BAND_BLOCK, 2 * BAND_BLOCK), 1)
    dist = BAND_BLOCK + qi - ki
    in_band = (dist >= 0) & (dist <= SPAN)
    first_mask = in_band & (ki >= jnp.where(b > 0, 0, BAND_BLOCK))

    def rows(ref, start):
        if dil == 1:
            return ref[pl.ds(start, BAND_BLOCK), :]
        return ref[pl.ds(start, BAND_BLOCK, stride=dil), :]

    def residue(r):
        for bi in range(nbb):
            base = bi * BAND_BLOCK * dil + r
            q = rows(q_ref, base).astype(BF16)
            if bi == 0:
                kp, vp = rows(kp_ref, r), rows(vp_ref, r)
            else:
                kp, vp = rows(kc_ref, base - BAND_BLOCK * dil), rows(vc_ref, base - BAND_BLOCK * dil)
            k = jnp.concatenate([kp, rows(kc_ref, base)], axis=0).astype(BF16)
            v = jnp.concatenate([vp, rows(vc_ref, base)], axis=0).astype(BF16)
            mask = first_mask if bi == 0 else in_band
            outs, lses = [], []
            for hl in range(LANES // HEAD_DIM):
                sl = slice(hl * HEAD_DIM, (hl + 1) * HEAD_DIM)
                s = lax.dot_general(q[:, sl], k[:, sl], (((1,), (1,)), ((), ())), preferred_element_type=F32)
                s = jnp.where(mask, s, NEG_INF)
                mx = jnp.max(s, axis=-1, keepdims=True)
                p = jnp.exp(s - mx)
                den = jnp.sum(p, axis=-1, keepdims=True)
                outs.append(jnp.dot(p.astype(BF16), v[:, sl], preferred_element_type=F32) / den)
                lses.append(jnp.broadcast_to(mx + jnp.log(den), (BAND_BLOCK, HEAD_DIM)))
            o = jnp.concatenate(outs, axis=1)
            l = jnp.concatenate(lses, axis=1)
            if dil == 1:
                o_ref[pl.ds(base, BAND_BLOCK), :] = o
                l_ref[pl.ds(base, BAND_BLOCK), :] = l
            else:
                o_ref[pl.ds(base, BAND_BLOCK, stride=dil), :] = o
                l_ref[pl.ds(base, BAND_BLOCK, stride=dil), :] = l

    if dil <= RESIDUE_UNROLL:
        for r in range(dil):
            residue(r)
    else:
        def residue_group(rg, carry):
            for ru in range(RESIDUE_UNROLL):
                residue(rg * RESIDUE_UNROLL + ru)
            return carry

        lax.fori_loop(0, dil // RESIDUE_UNROLL, residue_group, 0)


def _prompt_attention(qk, v, n, s, gi, dil):
    band_rows = BAND_BLOCK * dil
    nbb = max(512 // band_rows, 1)
    rows = band_rows * nbb
    qk4 = qk.reshape(qk.shape[0], n, s, LANES)
    v4 = v.reshape(v.shape[0], n, s, LANES)
    cur = (None, None, rows, LANES)
    prv = (None, None, band_rows, LANES)
    q0 = gi * GROUP_SLABS
    k0 = (N_GROUPS + gi) * GROUP_SLABS
    v0 = gi * GROUP_SLABS

    def prev(b):
        return jnp.maximum(b * nbb - 1, 0)

    in_specs = [
        pl.BlockSpec(cur, lambda a, c, b: (q0 + c, a, b, 0)),
        pl.BlockSpec(prv, lambda a, c, b: (k0 + c, a, prev(b), 0)),
        pl.BlockSpec(cur, lambda a, c, b: (k0 + c, a, b, 0)),
        pl.BlockSpec(prv, lambda a, c, b: (v0 + c, a, prev(b), 0)),
        pl.BlockSpec(cur, lambda a, c, b: (v0 + c, a, b, 0)),
    ]
    o_spec = pl.BlockSpec(cur, lambda a, c, b: (c, a, b, 0))
    out, lse = pl.pallas_call(
        functools.partial(_band_attn_kernel, dil=dil, nbb=nbb), grid=(n, GROUP_SLABS, s // rows),
        in_specs=in_specs, out_specs=[o_spec, o_spec],
        out_shape=[jax.ShapeDtypeStruct((GROUP_SLABS, n, s, LANES), F32)] * 2,
        compiler_params=_cparams(("parallel", "parallel", "arbitrary")), name=f"band_attn_d{dil}",
    )(qk4, qk4, qk4, v4, v4)
    return out.reshape(GROUP_SLABS, n * s, LANES), lse.reshape(GROUP_SLABS, n * s, LANES)


def _cached_attn_kernel(q_ref, c_ref, n_ref, bias_ref, co_ref, o_ref, l_ref, *, win, bb, n_new):
    bias = bias_ref[...]
    lane = lax.broadcasted_iota(jnp.int32, (HEAD_DIM, LANES), 1)
    is_new = lane >= LANES - n_new

    def body(b, carry):
        for h in range(N_SLOTS):
            hs, hl = h // 2, h % 2
            sl = slice(hl * HEAD_DIM, (hl + 1) * HEAD_DIM)
            q = q_ref[hs, b][:, sl]
            kt = jnp.concatenate([c_ref[b, 0, h], n_ref[b, 0, h]], axis=1)
            vt = jnp.concatenate([c_ref[b, 1, h], n_ref[b, 1, h]], axis=1)
            s = jnp.dot(q, kt, preferred_element_type=F32) + bias
            mx = jnp.max(s, axis=-1, keepdims=True)
            p = jnp.exp(s - mx)
            den = jnp.sum(p, axis=-1, keepdims=True)
            o = lax.dot_general(p, vt, (((1,), (1,)), ((), ())), preferred_element_type=F32) / den
            o_ref[hs, b, :, sl] = o
            l_ref[hs, b, :, sl] = jnp.broadcast_to(mx + jnp.log(den), (n_new, HEAD_DIM))
            for kv in range(2):
                rolled = pltpu.roll(c_ref[b, kv, h], win - n_new, 1)
                if win > LANES:
                    co_ref[b, kv, h, :, :win - LANES] = rolled[:, :win - LANES]
                co_ref[b, kv, h, :, win - LANES:] = jnp.where(is_new, n_ref[b, kv, h], rolled[:, win - LANES:])
        return carry

    lax.fori_loop(0, bb, body, 0)


def _cache_bias(win, dil, n_new):
    t = np.arange(n_new)[:, None]
    j = np.arange(win)[None, :]
    dist = win + t - j
    ok_c = (dist % dil == 0) & (dist <= dil * SPAN)
    l = np.arange(LANES)[None, :]
    tn = l - (LANES - n_new)
    ok_n = (tn >= 0) & (tn <= t) & ((t - tn) % dil == 0)
    ok = np.concatenate([ok_c, ok_n], axis=1)
    return jnp.asarray(np.where(ok, 0.0, NEG_INF), dtype=F32)


def _sample_attention(qk, new_t, cache, nb_, n_new, gi, win, dil):
    bb = max(1, min(8, 2048 // win))
    while nb_ % bb:
        bb //= 2
    qk4 = qk.reshape(qk.shape[0], nb_, n_new, LANES)
    cache_t = jnp.transpose(cache, (0, 2, 3, 4, 1))
    bias = _cache_bias(win, dil, n_new)
    c_blk = (bb, 2, N_SLOTS, HEAD_DIM, win)
    n_blk = (bb, 2, N_SLOTS, HEAD_DIM, LANES)
    q_blk = (GROUP_SLABS, bb, n_new, LANES)
    in_specs = [
        pl.BlockSpec(q_blk, lambda i: (gi, i, 0, 0)),
        pl.BlockSpec(c_blk, lambda i: (i, 0, 0, 0, 0)),
        pl.BlockSpec(n_blk, lambda i: (i, 0, 0, 0, 0)),
        pl.BlockSpec(bias.shape, lambda i: (0, 0)),
    ]
    o_spec = pl.BlockSpec(q_blk, lambda i: (0, i, 0, 0))
    cache_o, out, lse = pl.pallas_call(
        functools.partial(_cached_attn_kernel, win=win, bb=bb, n_new=n_new),
        grid=(nb_ // bb,),
        in_specs=in_specs,
        out_specs=[pl.BlockSpec(c_blk, lambda i: (i, 0, 0, 0, 0)), o_spec, o_spec],
        out_shape=[jax.ShapeDtypeStruct(cache_t.shape, F32),
                   jax.ShapeDtypeStruct((GROUP_SLABS, nb_, n_new, LANES), F32),
                   jax.ShapeDtypeStruct((GROUP_SLABS, nb_, n_new, LANES), F32)],
        compiler_params=_cparams(("parallel",)), name=f"cached_attn_d{dil}",
    )(qk4, cache_t, new_t, bias)
    new_cache = jnp.transpose(cache_o, (0, 4, 1, 2, 3))
    return (out.reshape(GROUP_SLABS, nb_ * n_new, LANES), lse.reshape(GROUP_SLABS, nb_ * n_new, LANES), new_cache)


def _layer_norm(x, g, b):
    mu = jnp.mean(x, axis=-1, keepdims=True)
    xc = x - mu
    var = jnp.mean(xc * xc, axis=-1, keepdims=True)
    return xc * lax.rsqrt(var + LN_EPS) * g + b


def _mix_kernel(x_ref, o0_ref, o1_ref, o2_ref, l0_ref, l1_ref, l2_ref, u_ref, h1_ref, h2_ref,
                cb_ref, ga_ref, gc_ref, cw_ref, wpa_ref, wpc_ref, wo_ref, g_ref, b_ref, out_ref,
                *, seg_len, alpha):
    tm = x_ref.shape[0]
    parts = []
    for c in range(GROUP_SLABS):
        l0, l1, l2 = l0_ref[c], l1_ref[c], l2_ref[c]
        lm = jnp.maximum(jnp.maximum(l0, l1), l2)
        w0, w1, w2 = jnp.exp(l0 - lm), jnp.exp(l1 - lm), jnp.exp(l2 - lm)
        parts.append((w0 * o0_ref[c] + w1 * o1_ref[c] + w2 * o2_ref[c]) / (w0 + w1 + w2))
    y_att = jnp.concatenate(parts, axis=1)
    u = u_ref[...]
    row = lax.broadcasted_iota(jnp.int32, u.shape, 0)
    if seg_len >= tm:
        keep = jnp.where((pl.program_id(0) * tm) % seg_len == 0, 0.0, 1.0).astype(F32)
        hrow1 = h1_ref[7:8, :] * keep
        hrow2 = h1_ref[6:7, :] * keep
        prev1 = jnp.where(row == 0, hrow1, pltpu.roll(u, 1, 0))
        prev2 = jnp.where(row == 0, hrow2, jnp.where(row == 1, hrow1, pltpu.roll(u, 2, 0)))
    else:
        rs = row & (seg_len - 1)
        prev1 = jnp.where(rs == 0, h1_ref[...], pltpu.roll(u, 1, 0))
        prev2 = jnp.where(rs < 2, h2_ref[...], pltpu.roll(u, 2, 0))
    cw = cw_ref[...]
    z_conv = prev2 * cw[0:1, :] + prev1 * cw[1:2, :] + u * cw[2:3, :]
    cv = cb_ref[...] * z_conv
    br_a = jnp.dot(y_att.astype(BF16), wpa_ref[...], preferred_element_type=F32)
    br_c = jnp.dot(cv.astype(BF16), wpc_ref[...], preferred_element_type=F32)
    merged = jax.nn.sigmoid(ga_ref[...]) * br_a + jax.nn.sigmoid(gc_ref[...]) * br_c
    mix = jnp.dot(merged.astype(BF16), wo_ref[...], preferred_element_type=F32)
    out_ref[...] = _layer_norm(alpha * x_ref[...] + mix, g_ref[...], b_ref[...])


def _mixer_finish(x, outs, lses, u, rest, hist1, hist2, seg_len, conv_w, wpa, wpc, wo, ln_g, ln_b, alpha):
    t = x.shape[0]
    tm = min(512, t)
    nt = t // tm
    row_spec = pl.BlockSpec((tm, D_MODEL), lambda i: (i, 0))
    att_spec = pl.BlockSpec((GROUP_SLABS, tm, LANES), lambda i: (0, i, 0))
    if seg_len >= tm:
        hb = tm // 8
        h1_spec = pl.BlockSpec((8, D_MODEL), lambda i: (jnp.maximum(i * hb - 1, 0), 0))
        h2_spec = h1_spec
    else:
        h1_spec = h2_spec = row_spec

    def full(shape):
        return pl.BlockSpec(shape, lambda i: (0,) * len(shape))

    in_specs = ([row_spec] + [att_spec] * 6 + [row_spec, h1_spec, h2_spec]
                + [pl.BlockSpec((tm, D_MODEL), lambda i, c=c: (i, c)) for c in range(3)]
                + [full((CONV_WIDTH, D_MODEL)), full((GROUP_W, D_MODEL)), full((D_MODEL, D_MODEL)),
                   full((D_MODEL, D_MODEL)), full((1, D_MODEL)), full((1, D_MODEL))])
    return pl.pallas_call(
        functools.partial(_mix_kernel, seg_len=seg_len, alpha=alpha),
        grid=(nt,), in_specs=in_specs, out_specs=row_spec,
        out_shape=jax.ShapeDtypeStruct((t, D_MODEL), F32),
        compiler_params=_cparams(("parallel",)), name="mixer_finish",
    )(x, *outs, *lses, u, hist1, hist2, rest, rest, rest, conv_w, wpa, wpc, wo, ln_g, ln_b)


def _ranked_top16(s):
    n_keys = s.shape[0]
    rowf = lax.broadcasted_iota(jnp.int32, s.shape, 0).astype(F32)
    rank = jnp.full(s.shape, float(PEER_N_KEYS - 1), F32)
    tops = []
    for k in range(PEER_TOPK):
        m = jnp.max(s, axis=0, keepdims=True)
        pos = jnp.min(jnp.where(s == m, rowf, float(n_keys)), axis=0, keepdims=True)
        hit = rowf == pos
        s = jnp.where(hit, -jnp.inf, s)
        rank = jnp.where(hit, float(k), rank)
        tops.append(m)
    return rank, tops


def _rows_to_tile(rows):
    tmw = rows[0].shape[1]
    ri = lax.broadcasted_iota(jnp.int32, (8, tmw), 0)
    out = jnp.zeros((8, tmw), F32)
    for a, rw in enumerate(rows):
        out = jnp.where(ri == a, rw, out)
    return out


def _select_counts(t1, t2):
    tmw = t1[0].shape[1]
    ri = lax.broadcasted_iota(jnp.int32, (8, tmw), 0)
    rif = ri.astype(F32)
    ninf = -jnp.inf
    v1lo, v1hi = _rows_to_tile(t1[:8]), _rows_to_tile(t1[8:])
    v2lo, v2hi = _rows_to_tile(t2[:8]), _rows_to_tile(t2[8:])
    all_rows = ri >= 0
    tiles = [
        (t1[0] + v2lo, rif, all_rows),
        (t1[0] + v2hi, 8.0 + rif, all_rows),
        (t1[1] + v2lo, 16.0 + rif, all_rows),
        (v1hi + t2[0], (8.0 + rif) * 16.0, all_rows),
        (v1lo + t2[0], rif * 16.0, ri >= 2),
        (v1lo + t2[1], rif * 16.0 + 1.0, ri >= 2),
        (t1[2] + v2lo, 32.0 + rif, (ri >= 2) & (ri <= 4)),
        (t1[3] + v2lo, 48.0 + rif, (ri >= 2) & (ri <= 3)),
        (t1[4] + v2lo, 64.0 + rif, ri == 2),
    ]
    vals = [jnp.where(ok, sm, ninf) for sm, _, ok in tiles]
    poss = [jnp.where(ok, ps, -1.0) for _, ps, ok in tiles]
    hits = [jnp.zeros((8, tmw), F32) for _ in vals]
    top = t1[0] + t2[0]
    z = jnp.zeros((1, tmw), F32)
    big = 1e9
    for _ in range(PEER_TOPK):
        m8 = vals[0]
        for vv in vals[1:]:
            m8 = jnp.maximum(m8, vv)
        m = jnp.max(m8, axis=0, keepdims=True)
        p8 = jnp.where(vals[0] == m, poss[0], big)
        for vv, pp in zip(vals[1:], poss[1:]):
            p8 = jnp.minimum(p8, jnp.where(vv == m, pp, big))
        pm = jnp.min(p8, axis=0, keepdims=True)
        for c in range(len(vals)):
            hit = poss[c] == pm
            vals[c] = jnp.where(hit, ninf, vals[c])
            hits[c] = jnp.where(hit, 1.0, hits[c])
        z = z + jnp.exp(m - top)

    def colsum(x):
        return jnp.sum(x, axis=0, keepdims=True)

    lo = hits[4] + hits[5]
    counts = [colsum(hits[0]) + colsum(hits[1]), colsum(hits[2]),
              lo[2:3] + colsum(hits[6]), lo[3:4] + colsum(hits[7]), lo[4:5] + colsum(hits[8]),
              lo[5:6], lo[6:7], lo[7:8]]
    counts += [hits[3][a:a + 1] for a in range(8)]
    return counts, z


def _packed_rows(row, n_rows):
    tile = jnp.broadcast_to(row, (16, row.shape[1])).astype(BF16)
    return jnp.concatenate([tile] * (n_rows // 16), axis=0)


def _peer_kernel(h_ref, wq_ref, k1_ref, k2_ref, u_ref, vt_ref, g_ref, b_ref, out_ref,
                 ht_ref, c1_ref, e1_ref, r2_ref, e2_ref, a_ref, wt_ref, ft_ref, *, alpha, eb):
    e = pl.program_id(1)
    n_e = pl.num_programs(1)
    half = PEER_D_KEY // 2
    rows_per_step = eb // PEER_N_KEYS

    @pl.when(e == 0)
    def _select():
        ht = h_ref[...].T.astype(BF16)
        ht_ref[...] = ht
        ft_ref[...] = jnp.zeros(ft_ref.shape, F32)

        def head(hh, carry):
            off = pl.multiple_of(hh * PEER_D_KEY, PEER_D_KEY)
            qy = jnp.dot(wq_ref[pl.ds(off, PEER_D_KEY), :], ht_ref[...], preferred_element_type=F32)
            s1 = jnp.dot(k1_ref[...], qy[:half].astype(BF16), preferred_element_type=F32)
            s2 = jnp.dot(k2_ref[...], qy[half:].astype(BF16), preferred_element_type=F32)
            r1, t1 = _ranked_top16(s1)
            r2, t2 = _ranked_top16(s2)
            counts, z = _select_counts(t1, t2)
            c1 = jnp.zeros(s1.shape, F32)
            for a in range(PEER_TOPK):
                c1 = jnp.where(r1 == float(a), counts[a], c1)
            c1_ref[hh] = c1
            e1_ref[hh] = jnp.exp(s1 - t1[0]) / z
            r2_ref[hh] = r2.astype(BF16)
            e2_ref[hh] = jnp.exp(s2 - t2[0]).astype(BF16)
            return carry

        lax.fori_loop(0, PEER_HEADS, head, 0)

    tmw = ht_ref.shape[1]
    cw = min(TOKEN_CHUNK, tmw)
    chunks = [slice(c0, c0 + cw) for c0 in range(0, tmw, cw)]
    for cs in chunks:
        a_ref[:, cs] = jnp.dot(u_ref[...], ht_ref[:, cs], preferred_element_type=F32)
    for cs in chunks:
        for il in range(rows_per_step):
            i = e * rows_per_step + il
            rs = slice(il * PEER_N_KEYS, (il + 1) * PEER_N_KEYS)
            pre = a_ref[rs, cs]
            act = 0.5 * pre * (1.0 + lax.erf(pre * math.sqrt(0.5)))
            gate = jnp.zeros(pre.shape, BF16)
            for hh in range(PEER_HEADS):
                cnt = _packed_rows(c1_ref[hh, pl.ds(i, 1), cs], PEER_N_KEYS)
                wgt = _packed_rows(e1_ref[hh, pl.ds(i, 1), cs], PEER_N_KEYS)
                gate = gate + jnp.where(r2_ref[hh, :, cs] < cnt, e2_ref[hh, :, cs], jnp.zeros((), BF16)) * wgt
            wt_ref[rs, cs] = gate * act.astype(BF16)
        ft_ref[:, cs] += jnp.dot(vt_ref[...], wt_ref[:, cs], preferred_element_type=F32)

    @pl.when(e == n_e - 1)
    def _finish():
        f = ft_ref[...].T
        out_ref[...] = _layer_norm(alpha * h_ref[...] + f, g_ref[...], b_ref[...])


def _peer(h, wq_t, k1, k2, u_bf, vt_bf, ln_g, ln_b, alpha):
    t = h.shape[0]
    tm = min(512, t)
    eb = 1024
    n_exp = u_bf.shape[0]
    row_spec = pl.BlockSpec((tm, D_MODEL), lambda i, e: (i, 0))

    def full(shape):
        return pl.BlockSpec(shape, lambda i, e: (0,) * len(shape))

    meta32 = pltpu.VMEM((PEER_HEADS, PEER_N_KEYS, tm), F32)
    meta16 = pltpu.VMEM((PEER_HEADS, PEER_N_KEYS, tm), BF16)
    return pl.pallas_call(
        functools.partial(_peer_kernel, alpha=alpha, eb=eb),
        grid=(t // tm, n_exp // eb),
        in_specs=[row_spec, full(wq_t.shape), full(k1.shape), full(k2.shape),
                  pl.BlockSpec((eb, D_MODEL), lambda i, e: (e, 0)),
                  pl.BlockSpec((D_MODEL, eb), lambda i, e: (0, e)),
                  full((1, D_MODEL)), full((1, D_MODEL))],
        out_specs=row_spec,
        out_shape=jax.ShapeDtypeStruct((t, D_MODEL), F32),
        scratch_shapes=[pltpu.VMEM((D_MODEL, tm), BF16), meta32, meta32, meta16, meta16,
                        pltpu.VMEM((eb, tm), F32), pltpu.VMEM((eb, tm), BF16), pltpu.VMEM((D_MODEL, tm), F32)],
        compiler_params=_cparams(("parallel", "arbitrary")), name="peer",
    )(h, wq_t, k1, k2, u_bf, vt_bf, ln_g, ln_b)


def _rope_tables(pos):
    half = HEAD_DIM // 2
    inv_freq = ROPE_THETA ** (-jnp.arange(half, dtype=F32) / half)
    ang = pos.astype(F32)[:, None] * inv_freq[None, :]
    cos, sin = jnp.cos(ang), jnp.sin(ang)
    cos_t = jnp.concatenate([cos, cos, cos, cos], axis=-1)
    sin_t = jnp.concatenate([-sin, sin, -sin, sin], axis=-1)
    return cos_t, sin_t


def _slab_rows_to_heads(slabs, lead):
    nl = len(lead)
    x = jnp.moveaxis(slabs, 0, nl + 1)
    return x.reshape(*lead, x.shape[nl], N_SLOTS, HEAD_DIM)


def kernel(x_prompt, x_sample, cache_kv_w128, cache_kv_w512, cache_kv_w2048, state_conv, w_in, conv_w, w_branch_attn, w_branch_conv, w_out, ln1_g, ln1_b, peer_w_q, peer_keys_1, peer_keys_2, peer_u, peer_v, ln2_g, ln2_b):
    depth = w_in.shape[0]
    assert depth == 1, "single-layer trunk"
    n, s, d = x_prompt.shape
    nb_, n_new, _ = x_sample.shape
    past_len = 8192
    alpha = (2.0 * depth) ** 0.25
    caches = (cache_kv_w128, cache_kv_w512, cache_kv_w2048)
    for (win, dil), c in zip(DIL_GROUPS, caches):
        assert c.shape[2] == win and s % (dil * BAND_BLOCK) == 0 and win // dil == SPAN and win <= s

    lyr = 0
    w_bf = w_in[lyr].astype(BF16)
    wpa = w_branch_attn[lyr].astype(BF16)
    wpc = w_branch_conv[lyr].astype(BF16)
    wo = w_out[lyr].astype(BF16)
    wq_t = peer_w_q[lyr].T.astype(BF16)
    k1 = peer_keys_1[lyr].astype(BF16)
    k2 = peer_keys_2[lyr].astype(BF16)
    u_bf = peer_u[lyr].astype(BF16)
    vt_bf = peer_v[lyr].T.astype(BF16)
    g1, b1 = ln1_g[lyr][None, :], ln1_b[lyr][None, :]
    g2, b2 = ln2_g[lyr][None, :], ln2_b[lyr][None, :]
    cw = conv_w[lyr]
    k_slab0 = N_GROUPS * GROUP_SLABS

    xp = x_prompt.reshape(n * s, d)
    cos_p, sin_p = _rope_tables(jnp.arange(s))
    qk_p, v_p, u_p, rest_p = _input_projections(xp, w_bf, cos_p, sin_p)
    outs, lses = [], []
    for gi, (win, dil) in enumerate(DIL_GROUPS):
        o, l = _prompt_attention(qk_p, v_p, n, s, gi, dil)
        outs.append(o)
        lses.append(l)
    h_p = _mixer_finish(xp, outs, lses, u_p, rest_p, u_p, u_p, s, cw, wpa, wpc, wo, g1, b1, alpha)
    y_p = _peer(h_p, wq_t, k1, k2, u_bf, vt_bf, g2, b2, alpha).reshape(n, s, d)

    qk_p4 = qk_p.reshape(qk_p.shape[0], n, s, LANES)
    v_p4 = v_p.reshape(v_p.shape[0], n, s, LANES)
    kv_p = []
    for gi, (win, dil) in enumerate(DIL_GROUPS):
        ks = k_slab0 + gi * GROUP_SLABS
        vs = gi * GROUP_SLABS
        k_tail = _slab_rows_to_heads(qk_p4[ks:ks + GROUP_SLABS, :, s - win:], (n,))
        v_tail = _slab_rows_to_heads(v_p4[vs:vs + GROUP_SLABS, :, s - win:], (n,))
        kv_p.append(jnp.stack([k_tail, v_tail], axis=2)[None])
    conv_state_p = u_p.reshape(n, s, d)[:, s - (CONV_WIDTH - 1):][None]

    xs = x_sample.reshape(nb_ * n_new, d)
    cos_s, sin_s = _rope_tables(jnp.tile(past_len + jnp.arange(n_new), nb_))
    qk_s, v_s, u_s, rest_s = _input_projections(xs, w_bf, cos_s, sin_s)
    qk_s4 = qk_s.reshape(qk_s.shape[0], nb_, n_new, LANES)
    v_s4 = v_s.reshape(v_s.shape[0], nb_, n_new, LANES)
    outs, lses, kv_s = [], [], []
    for gi, (win, dil) in enumerate(DIL_GROUPS):
        ks = k_slab0 + gi * GROUP_SLABS
        vs = gi * GROUP_SLABS
        k_new = _slab_rows_to_heads(qk_s4[ks:ks + GROUP_SLABS], (nb_,))
        v_new = _slab_rows_to_heads(v_s4[vs:vs + GROUP_SLABS], (nb_,))
        new_t = jnp.transpose(jnp.stack([k_new, v_new], axis=1), (0, 1, 3, 4, 2))
        new_t = jnp.pad(new_t, ((0, 0),) * 4 + ((LANES - n_new, 0),))
        o, l, cache_new = _sample_attention(qk_s, new_t, caches[gi][lyr], nb_, n_new, gi, win, dil)
        outs.append(o)
        lses.append(l)
        kv_s.append(cache_new[None])
    st = state_conv[lyr]
    zeros6 = jnp.zeros((nb_, n_new - 1, d), F32)
    hist1 = jnp.concatenate([st[:, 1:2], zeros6], axis=1).reshape(nb_ * n_new, d)
    hist2 = jnp.concatenate([st, zeros6[:, 1:]], axis=1).reshape(nb_ * n_new, d)
    h_s = _mixer_finish(xs, outs, lses, u_s, rest_s, hist1, hist2, n_new, cw, wpa, wpc, wo, g1, b1, alpha)
    y_s = _peer(h_s, wq_t, k1, k2, u_bf, vt_bf, g2, b2, alpha).reshape(nb_, n_new, d)

    u_ext_s = jnp.concatenate([st, u_s.reshape(nb_, n_new, d)], axis=1)
    conv_state_s = u_ext_s[:, u_ext_s.shape[1] - (CONV_WIDTH - 1):][None]

    return (y_p, y_s, kv_p[0], kv_p[1], kv_p[2], conv_state_p,
            kv_s[0], kv_s[1], kv_s[2], conv_state_s)
```

```python
import functools
import math

import numpy as np
import jax
import jax.numpy as jnp
from jax import lax
from jax.experimental import pallas as pl
from jax.experimental.pallas import tpu as pltpu

F32 = jnp.float32
BF16 = jnp.bfloat16

LANES = 128
MXU_WIDTH = 256
D_MODEL = 1024
N_GROUPS = 3
N_SLOTS = 8
HEAD_DIM = 64
GROUP_W = N_SLOTS * HEAD_DIM
GROUP_SLABS = GROUP_W // LANES
ATT_W = N_GROUPS * GROUP_W
DIL_GROUPS = ((128, 1), (512, 4), (2048, 16))
SPAN = 128
BAND_BLOCK = 128
ROPE_THETA = 10000.0
CONV_WIDTH = 3
PEER_HEADS = 8
PEER_N_KEYS = 128
PEER_D_KEY = 256
PEER_TOPK = 16
LN_EPS = 1e-5
NEG_INF = -1e30
TOKEN_CHUNK = 256
RESIDUE_UNROLL = 4

COL_TILE = 512
_QK_TILE0, _V_TILE0, _CIN_TILE0, _CB_TILE0, _CC_TILE0, _GA_TILE0 = 0, 6, 9, 11, 13, 15

VMEM_LIMIT = 56 * 1024 * 1024


def _cparams(sem):
    return pltpu.CompilerParams(dimension_semantics=sem, vmem_limit_bytes=VMEM_LIMIT)


def _proj_rope_kernel(x_ref, w_ref, cos_ref, sin_ref, o_ref, *, n_q_tiles):
    j = pl.program_id(1)
    xb = x_ref[...].astype(BF16)
    cos = cos_ref[...]
    sin = sin_ref[...]
    lane = lax.broadcasted_iota(jnp.int32, cos.shape, 1)
    first_half = (lane & (HEAD_DIM // 2)) == 0
    scale = jnp.where(j < n_q_tiles, HEAD_DIM ** -0.5, 1.0).astype(F32)
    for half in range(COL_TILE // MXU_WIDTH):
        acc = jnp.dot(xb, w_ref[:, half * MXU_WIDTH:(half + 1) * MXU_WIDTH], preferred_element_type=F32)
        for c in range(MXU_WIDTH // LANES):
            a = acc[:, c * LANES:(c + 1) * LANES]
            partner = jnp.where(first_half, pltpu.roll(a, LANES - HEAD_DIM // 2, 1), pltpu.roll(a, HEAD_DIM // 2, 1))
            o_ref[half * (MXU_WIDTH // LANES) + c] = (a * cos + partner * sin) * scale


def _proj_slab_kernel(x_ref, w_ref, o_ref):
    acc = jnp.dot(x_ref[...].astype(BF16), w_ref[...], preferred_element_type=F32)
    for c in range(COL_TILE // LANES):
        o_ref[c] = acc[:, c * LANES:(c + 1) * LANES]


def _proj_kernel(x_ref, w_ref, o_ref):
    o_ref[...] = jnp.dot(x_ref[...].astype(BF16), w_ref[...], preferred_element_type=F32)


def _proj_product_kernel(x_ref, w1_ref, w2_ref, o_ref):
    xb = x_ref[...].astype(BF16)
    o_ref[...] = (jnp.dot(xb, w1_ref[...], preferred_element_type=F32)
                  * jnp.dot(xb, w2_ref[...], preferred_element_type=F32))


def _input_projections(x, w_bf, cos_t, sin_t):
    t = x.shape[0]
    tm = min(1024, t)
    nt = t // tm
    n_pos_tiles = cos_t.shape[0] // tm
    x_spec = pl.BlockSpec((tm, D_MODEL), lambda i, j: (i, 0))

    def w_spec(col_fn):
        return pl.BlockSpec((D_MODEL, COL_TILE), lambda i, j: (0, col_fn(j)))

    o_spec = pl.BlockSpec((tm, COL_TILE), lambda i, j: (i, j))
    slab_spec = pl.BlockSpec((COL_TILE // LANES, tm, LANES), lambda i, j: (j, i, 0))
    pos_spec = pl.BlockSpec((tm, LANES), lambda i, j: (i % n_pos_tiles, 0))
    sem = ("parallel", "arbitrary")

    n_qk = 2 * ATT_W // COL_TILE
    qk = pl.pallas_call(
        functools.partial(_proj_rope_kernel, n_q_tiles=ATT_W // COL_TILE),
        grid=(nt, n_qk),
        in_specs=[x_spec, w_spec(lambda j: j + _QK_TILE0), pos_spec, pos_spec],
        out_specs=slab_spec,
        out_shape=jax.ShapeDtypeStruct((2 * ATT_W // LANES, t, LANES), F32),
        compiler_params=_cparams(sem), name="proj_qk_rope",
    )(x, w_bf, cos_t, sin_t)

    v = pl.pallas_call(
        _proj_slab_kernel, grid=(nt, ATT_W // COL_TILE),
        in_specs=[x_spec, w_spec(lambda j: j + _V_TILE0)],
        out_specs=slab_spec, out_shape=jax.ShapeDtypeStruct((ATT_W // LANES, t, LANES), F32),
        compiler_params=_cparams(sem), name="proj_v",
    )(x, w_bf)

    u = pl.pallas_call(
        _proj_product_kernel, grid=(nt, D_MODEL // COL_TILE),
        in_specs=[x_spec, w_spec(lambda j: j + _CIN_TILE0), w_spec(lambda j: j + _CC_TILE0)],
        out_specs=o_spec, out_shape=jax.ShapeDtypeStruct((t, D_MODEL), F32),
        compiler_params=_cparams(sem), name="proj_conv_u",
    )(x, w_bf, w_bf)

    rest = pl.pallas_call(
        _proj_kernel, grid=(nt, 3 * D_MODEL // COL_TILE),
        in_specs=[x_spec, w_spec(lambda j: jnp.where(j < 2, j + _CB_TILE0, j - 2 + _GA_TILE0))],
        out_specs=o_spec, out_shape=jax.ShapeDtypeStruct((t, 3 * D_MODEL), F32),
        compiler_params=_cparams(sem), name="proj_cb_gates",
    )(x, w_bf)
    return qk, v, u, rest


def _band_attn_kernel(q_ref, kp_ref, kc_ref, vp_ref, vc_ref, o_ref, l_ref, *, dil, nbb):
    b = pl.program_id(2)
    qi = lax.broadcasted_iota(jnp.int32, (BAND_BLOCK, 2 * BAND_BLOCK), 0)
    ki = lax.broadcasted_iota(jnp.int32, (BAND_BLOCK, 2 * BAND_BLOCK), 1)
    dist = BAND_BLOCK + qi - ki
    in_band = (dist >= 0) & (dist <= SPAN)
    first_mask = in_band & (ki >= jnp.where(b > 0, 0, BAND_BLOCK))

    def rows(ref, start):
        if dil == 1:
            return ref[pl.ds(start, BAND_BLOCK), :]
        return ref[pl.ds(start, BAND_BLOCK, stride=dil), :]

    def residue(r):
        for bi in range(nbb):
            base = bi * BAND_BLOCK * dil + r
            q = rows(q_ref, base).astype(BF16)
            if bi == 0:
                kp, vp = rows(kp_ref, r), rows(vp_ref, r)
            else:
                kp, vp = rows(kc_ref, base - BAND_BLOCK * dil), rows(vc_ref, base - BAND_BLOCK * dil)
            k = jnp.concatenate([kp, rows(kc_ref, base)], axis=0).astype(BF16)
            v = jnp.concatenate([vp, rows(vc_ref, base)], axis=0).astype(BF16)
            mask = first_mask if bi == 0 else in_band
            outs, lses = [], []
            for hl in range(LANES // HEAD_DIM):
                sl = slice(hl * HEAD_DIM, (hl + 1) * HEAD_DIM)
                s = lax.dot_general(q[:, sl], k[:, sl], (((1,), (1,)), ((), ())), preferred_element_type=F32)
                s = jnp.where(mask, s, NEG_INF)
                mx = jnp.max(s, axis=-1, keepdims=True)
                p = jnp.exp(s - mx)
                den = jnp.sum(p, axis=-1, keepdims=True)
                outs.append(jnp.dot(p.astype(BF16), v[:, sl], preferred_element_type=F32) / den)
                lses.append(jnp.broadcast_to(mx + jnp.log(den), (BAND_BLOCK, HEAD_DIM)))
            o = jnp.concatenate(outs, axis=1)
            l = jnp.concatenate(lses, axis=1)
            if dil == 1:
                o_ref[pl.ds(base, BAND_BLOCK), :] = o
                l_ref[pl.ds(base, BAND_BLOCK), :] = l
            else:
                o_ref[pl.ds(base, BAND_BLOCK, stride=dil), :] = o
                l_ref[pl.ds(base, BAND_BLOCK, stride=dil), :] = l

    if dil <= RESIDUE_UNROLL:
        for r in range(dil):
            residue(r)
    else:
        def residue_group(rg, carry):
            for ru in range(RESIDUE_UNROLL):
                residue(rg * RESIDUE_UNROLL + ru)
            return carry

        lax.fori_loop(0, dil // RESIDUE_UNROLL, residue_group, 0)


def _prompt_attention(qk, v, n, s, gi, dil):
    band_rows = BAND_BLOCK * dil
    nbb = max(512 // band_rows, 1)
    rows = band_rows * nbb
    qk4 = qk.reshape(qk.shape[0], n, s, LANES)
    v4 = v.reshape(v.shape[0], n, s, LANES)
    cur = (None, None, rows, LANES)
    prv = (None, None, band_rows, LANES)
    q0 = gi * GROUP_SLABS
    k0 = (N_GROUPS + gi) * GROUP_SLABS
    v0 = gi * GROUP_SLABS

    def prev(b):
        return jnp.maximum(b * nbb - 1, 0)

    in_specs = [
        pl.BlockSpec(cur, lambda a, c, b: (q0 + c, a, b, 0)),
        pl.BlockSpec(prv, lambda a, c, b: (k0 + c, a, prev(b), 0)),
        pl.BlockSpec(cur, lambda a, c, b: (k0 + c, a, b, 0)),
        pl.BlockSpec(prv, lambda a, c, b: (v0 + c, a, prev(b), 0)),
        pl.BlockSpec(cur, lambda a, c, b: (v0 + c, a, b, 0)),
    ]
    o_spec = pl.BlockSpec(cur, lambda a, c, b: (c, a, b, 0))
    out, lse = pl.pallas_call(
        functools.partial(_band_attn_kernel, dil=dil, nbb=nbb), grid=(n, GROUP_SLABS, s // rows),
        in_specs=in_specs, out_specs=[o_spec, o_spec],
        out_shape=[jax.ShapeDtypeStruct((GROUP_SLABS, n, s, LANES), F32)] * 2,
        compiler_params=_cparams(("parallel", "parallel", "arbitrary")), name=f"band_attn_d{dil}",
    )(qk4, qk4, qk4, v4, v4)
    return out.reshape(GROUP_SLABS, n * s, LANES), lse.reshape(GROUP_SLABS, n * s, LANES)


def _cached_attn_kernel(q_ref, c_ref, n_ref, bias_ref, co_ref, o_ref, l_ref, *, win, bb, n_new):
    bias = bias_ref[...]
    lane = lax.broadcasted_iota(jnp.int32, (HEAD_DIM, LANES), 1)
    is_new = lane >= LANES - n_new

    def body(b, carry):
        for h in range(N_SLOTS):
            hs, hl = h // 2, h % 2
            sl = slice(hl * HEAD_DIM, (hl + 1) * HEAD_DIM)
            q = q_ref[hs, b][:, sl]
            kt = jnp.concatenate([c_ref[b, 0, h], n_ref[b, 0, h]], axis=1)
            vt = jnp.concatenate([c_ref[b, 1, h], n_ref[b, 1, h]], axis=1)
            s = jnp.dot(q, kt, preferred_element_type=F32) + bias
            mx = jnp.max(s, axis=-1, keepdims=True)
            p = jnp.exp(s - mx)
            den = jnp.sum(p, axis=-1, keepdims=True)
            o = lax.dot_general(p, vt, (((1,), (1,)), ((), ())), preferred_element_type=F32) / den
            o_ref[hs, b, :, sl] = o
            l_ref[hs, b, :, sl] = jnp.broadcast_to(mx + jnp.log(den), (n_new, HEAD_DIM))
            for kv in range(2):
                rolled = pltpu.roll(c_ref[b, kv, h], win - n_new, 1)
                if win > LANES:
                    co_ref[b, kv, h, :, :win - LANES] = rolled[:, :win - LANES]
                co_ref[b, kv, h, :, win - LANES:] = jnp.where(is_new, n_ref[b, kv, h], rolled[:, win - LANES:])
        return carry

    lax.fori_loop(0, bb, body, 0)


def _cache_bias(win, dil, n_new):
    t = np.arange(n_new)[:, None]
    j = np.arange(win)[None, :]
    dist = win + t - j
    ok_c = (dist % dil == 0) & (dist <= dil * SPAN)
    l = np.arange(LANES)[None, :]
    tn = l - (LANES - n_new)
    ok_n = (tn >= 0) & (tn <= t) & ((t - tn) % dil == 0)
    ok = np.concatenate([ok_c, ok_n], axis=1)
    return jnp.asarray(np.where(ok, 0.0, NEG_INF), dtype=F32)


def _sample_attention(qk, new_t, cache, nb_, n_new, gi, win, dil):
    bb = max(1, min(8, 2048 // win))
    while nb_ % bb:
        bb //= 2
    qk4 = qk.reshape(qk.shape[0], nb_, n_new, LANES)
    cache_t = jnp.transpose(cache, (0, 2, 3, 4, 1))
    bias = _cache_bias(win, dil, n_new)
    c_blk = (bb, 2, N_SLOTS, HEAD_DIM, win)
    n_blk = (bb, 2, N_SLOTS, HEAD_DIM, LANES)
    q_blk = (GROUP_SLABS, bb, n_new, LANES)
    in_specs = [
        pl.BlockSpec(q_blk, lambda i: (gi, i, 0, 0)),
        pl.BlockSpec(c_blk, lambda i: (i, 0, 0, 0, 0)),
        pl.BlockSpec(n_blk, lambda i: (i, 0, 0, 0, 0)),
        pl.BlockSpec(bias.shape, lambda i: (0, 0)),
    ]
    o_spec = pl.BlockSpec(q_blk, lambda i: (0, i, 0, 0))
    cache_o, out, lse = pl.pallas_call(
        functools.partial(_cached_attn_kernel, win=win, bb=bb, n_new=n_new),
        grid=(nb_ // bb,),
        in_specs=in_specs,
        out_specs=[pl.BlockSpec(c_blk, lambda i: (i, 0, 0, 0, 0)), o_spec, o_spec],
        out_shape=[jax.ShapeDtypeStruct(cache_t.shape, F32),
                   jax.ShapeDtypeStruct((GROUP_SLABS, nb_, n_new, LANES), F32),
                   jax.ShapeDtypeStruct((GROUP_SLABS, nb_, n_new, LANES), F32)],
        compiler_params=_cparams(("parallel",)), name=f"cached_attn_d{dil}",
    )(qk4, cache_t, new_t, bias)
    new_cache = jnp.transpose(cache_o, (0, 4, 1, 2, 3))
    return (out.reshape(GROUP_SLABS, nb_ * n_new, LANES), lse.reshape(GROUP_SLABS, nb_ * n_new, LANES), new_cache)


def _layer_norm(x, g, b):
    mu = jnp.mean(x, axis=-1, keepdims=True)
    xc = x - mu
    var = jnp.mean(xc * xc, axis=-1, keepdims=True)
    return xc * lax.rsqrt(var + LN_EPS) * g + b


def _mix_kernel(x_ref, o0_ref, o1_ref, o2_ref, l0_ref, l1_ref, l2_ref, u_ref, h1_ref, h2_ref,
                cb_ref, ga_ref, gc_ref, cw_ref, wpa_ref, wpc_ref, wo_ref, g_ref, b_ref, out_ref,
                *, seg_len, alpha):
    tm = x_ref.shape[0]
    parts = []
    for c in range(GROUP_SLABS):
        l0, l1, l2 = l0_ref[c], l1_ref[c], l2_ref[c]
        lm = jnp.maximum(jnp.maximum(l0, l1), l2)
        w0, w1, w2 = jnp.exp(l0 - lm), jnp.exp(l1 - lm), jnp.exp(l2 - lm)
        parts.append((w0 * o0_ref[c] + w1 * o1_ref[c] + w2 * o2_ref[c]) / (w0 + w1 + w2))
    y_att = jnp.concatenate(parts, axis=1)
    u = u_ref[...]
    row = lax.broadcasted_iota(jnp.int32, u.shape, 0)
    if seg_len >= tm:
        keep = jnp.where((pl.program_id(0) * tm) % seg_len == 0, 0.0, 1.0).astype(F32)
        hrow1 = h1_ref[7:8, :] * keep
        hrow2 = h1_ref[6:7, :] * keep
        prev1 = jnp.where(row == 0, hrow1, pltpu.roll(u, 1, 0))
        prev2 = jnp.where(row == 0, hrow2, jnp.where(row == 1, hrow1, pltpu.roll(u, 2, 0)))
    else:
        rs = row & (seg_len - 1)
        prev1 = jnp.where(rs == 0, h1_ref[...], pltpu.roll(u, 1, 0))
        prev2 = jnp.where(rs < 2, h2_ref[...], pltpu.roll(u, 2, 0))
    cw = cw_ref[...]
    z_conv = prev2 * cw[0:1, :] + prev1 * cw[1:2, :] + u * cw[2:3, :]
    cv = cb_ref[...] * z_conv
    br_a = jnp.dot(y_att.astype(BF16), wpa_ref[...], preferred_element_type=F32)
    br_c = jnp.dot(cv.astype(BF16), wpc_ref[...], preferred_element_type=F32)
    merged = jax.nn.sigmoid(ga_ref[...]) * br_a + jax.nn.sigmoid(gc_ref[...]) * br_c
    mix = jnp.dot(merged.astype(BF16), wo_ref[...], preferred_element_type=F32)
    out_ref[...] = _layer_norm(alpha * x_ref[...] + mix, g_ref[...], b_ref[...])


def _mixer_finish(x, outs, lses, u, rest, hist1, hist2, seg_len, conv_w, wpa, wpc, wo, ln_g, ln_b, alpha):
    t = x.shape[0]
    tm = min(512, t)
    nt = t // tm
    row_spec = pl.BlockSpec((tm, D_MODEL), lambda i: (i, 0))
    att_spec = pl.BlockSpec((GROUP_SLABS, tm, LANES), lambda i: (0, i, 0))
    if seg_len >= tm:
        hb = tm // 8
        h1_spec = pl.BlockSpec((8, D_MODEL), lambda i: (jnp.maximum(i * hb - 1, 0), 0))
        h2_spec = h1_spec
    else:
        h1_spec = h2_spec = row_spec

    def full(shape):
        return pl.BlockSpec(shape, lambda i: (0,) * len(shape))

    in_specs = ([row_spec] + [att_spec] * 6 + [row_spec, h1_spec, h2_spec]
                + [pl.BlockSpec((tm, D_MODEL), lambda i, c=c: (i, c)) for c in range(3)]
                + [full((CONV_WIDTH, D_MODEL)), full((GROUP_W, D_MODEL)), full((D_MODEL, D_MODEL)),
                   full((D_MODEL, D_MODEL)), full((1, D_MODEL)), full((1, D_MODEL))])
    return pl.pallas_call(
        functools.partial(_mix_kernel, seg_len=seg_len, alpha=alpha),
        grid=(nt,), in_specs=in_specs, out_specs=row_spec,
        out_shape=jax.ShapeDtypeStruct((t, D_MODEL), F32),
        compiler_params=_cparams(("parallel",)), name="mixer_finish",
    )(x, *outs, *lses, u, hist1, hist2, rest, rest, rest, conv_w, wpa, wpc, wo, ln_g, ln_b)


def _ranked_top16(s):
    n_keys = s.shape[0]
    rowf = lax.broadcasted_iota(jnp.int32, s.shape, 0).astype(F32)
    rank = jnp.full(s.shape, float(PEER_N_KEYS - 1), F32)
    tops = []
    for k in range(PEER_TOPK):
        m = jnp.max(s, axis=0, keepdims=True)
        pos = jnp.min(jnp.where(s == m, rowf, float(n_keys)), axis=0, keepdims=True)
        hit = rowf == pos
        s = jnp.where(hit, -jnp.inf, s)
        rank = jnp.where(hit, float(k), rank)
        tops.append(m)
    return rank, tops


def _rows_to_tile(rows):
    tmw = rows[0].shape[1]
    ri = lax.broadcasted_iota(jnp.int32, (8, tmw), 0)
    out = jnp.zeros((8, tmw), F32)
    for a, rw in enumerate(rows):
        out = jnp.where(ri == a, rw, out)
    return out


def _select_counts(t1, t2):
    tmw = t1[0].shape[1]
    ri = lax.broadcasted_iota(jnp.int32, (8, tmw), 0)
    rif = ri.astype(F32)
    ninf = -jnp.inf
    v1lo, v1hi = _rows_to_tile(t1[:8]), _rows_to_tile(t1[8:])
    v2lo, v2hi = _rows_to_tile(t2[:8]), _rows_to_tile(t2[8:])
    all_rows = ri >= 0
    tiles = [
        (t1[0] + v2lo, rif, all_rows),
        (t1[0] + v2hi, 8.0 + rif, all_rows),
        (t1[1] + v2lo, 16.0 + rif, all_rows),
        (v1hi + t2[0], (8.0 + rif) * 16.0, all_rows),
        (v1lo + t2[0], rif * 16.0, ri >= 2),
        (v1lo + t2[1], rif * 16.0 + 1.0, ri >= 2),
        (t1[2] + v2lo, 32.0 + rif, (ri >= 2) & (ri <= 4)),
        (t1[3] + v2lo, 48.0 + rif, (ri >= 2) & (ri <= 3)),
        (t1[4] + v2lo, 64.0 + rif, ri == 2),
    ]
    vals = [jnp.where(ok, sm, ninf) for sm, _, ok in tiles]
    poss = [jnp.where(ok, ps, -1.0) for _, ps, ok in tiles]
    hits = [jnp.zeros((8, tmw), F32) for _ in vals]
    top = t1[0] + t2[0]
    z = jnp.zeros((1, tmw), F32)
    big = 1e9
    for _ in range(PEER_TOPK):
        m8 = vals[0]
        for vv in vals[1:]:
            m8 = jnp.maximum(m8, vv)
        m = jnp.max(m8, axis=0, keepdims=True)
        p8 = jnp.where(vals[0] == m, poss[0], big)
        for vv, pp in zip(vals[1:], poss[1:]):
            p8 = jnp.minimum(p8, jnp.where(vv == m, pp, big))
        pm = jnp.min(p8, axis=0, keepdims=True)
        for c in range(len(vals)):
            hit = poss[c] == pm
            vals[c] = jnp.where(hit, ninf, vals[c])
            hits[c] = jnp.where(hit, 1.0, hits[c])
        z = z + jnp.exp(m - top)

    def colsum(x):
        return jnp.sum(x, axis=0, keepdims=True)

    lo = hits[4] + hits[5]
    counts = [colsum(hits[0]) + colsum(hits[1]), colsum(hits[2]),
              lo[2:3] + colsum(hits[6]), lo[3:4] + colsum(hits[7]), lo[4:5] + colsum(hits[8]),
              lo[5:6], lo[6:7], lo[7:8]]
    counts += [hits[3][a:a + 1] for a in range(8)]
    return counts, z


def _packed_rows(row, n_rows):
    tile = jnp.broadcast_to(row, (16, row.shape[1])).astype(BF16)
    return jnp.concatenate([tile] * (n_rows // 16), axis=0)


def _peer_kernel(hn_ref, hp_ref, wq_ref, k1_ref, k2_ref, u_ref, vt_ref, g_ref, b_ref, out_ref,
                 ht_ref, c1_ref, e1_ref, r2_ref, e2_ref, a_ref, wt_ref, ft_ref, *, alpha, eb, n_chunks):
    i = pl.program_id(0)
    e = pl.program_id(1)
    n_e = pl.num_programs(1)
    half = PEER_D_KEY // 2
    rows_per_step = eb // PEER_N_KEYS
    cw = ht_ref.shape[-1]
    sel_slot = i % 2
    use_slot = (i + 1) % 2

    @pl.when((i == 0) & (e == 0))
    def _init():
        ht_ref[1] = jnp.zeros(ht_ref.shape[1:], BF16)
        c1_ref[1] = jnp.zeros(c1_ref.shape[1:], F32)
        e1_ref[1] = jnp.zeros(e1_ref.shape[1:], F32)
        r2_ref[1] = jnp.zeros(r2_ref.shape[1:], BF16)
        e2_ref[1] = jnp.zeros(e2_ref.shape[1:], BF16)

    @pl.when(e == 0)
    def _start_tile():
        ht = hn_ref[...].T.astype(BF16)
        for c in range(n_chunks):
            ht_ref[sel_slot, c] = ht[:, c * cw:(c + 1) * cw]
        ft_ref[...] = jnp.zeros(ft_ref.shape, F32)

    part = e % (PEER_HEADS * n_chunks)
    hh = part // n_chunks
    sc = part % n_chunks
    off = pl.multiple_of(hh * PEER_D_KEY, PEER_D_KEY)
    qy = jnp.dot(wq_ref[pl.ds(off, PEER_D_KEY), :], ht_ref[sel_slot, sc], preferred_element_type=F32)
    s1 = jnp.dot(k1_ref[...], qy[:half].astype(BF16), preferred_element_type=F32)
    s2 = jnp.dot(k2_ref[...], qy[half:].astype(BF16), preferred_element_type=F32)

    def select_column(l0):
        ls = slice(l0, l0 + LANES)
        s1c, s2c = s1[:, ls], s2[:, ls]
        e1_ref[sel_slot, sc, hh, :, ls] = jnp.exp(s1c - jnp.max(s1c, axis=0, keepdims=True))
        e2_ref[sel_slot, sc, hh, :, ls] = jnp.exp(s2c - jnp.max(s2c, axis=0, keepdims=True)).astype(BF16)
        r2, t2 = _ranked_top16(s2c)
        r2_ref[sel_slot, sc, hh, :, ls] = r2.astype(BF16)
        r1, t1 = _ranked_top16(s1c)
        counts, z = _select_counts(t1, t2)
        c1 = jnp.zeros(s1c.shape, F32)
        for a in range(PEER_TOPK):
            c1 = jnp.where(r1 == float(a), counts[a], c1)
        c1_ref[sel_slot, sc, hh, :, ls] = c1
        e1_ref[sel_slot, sc, hh, :, ls] = e1_ref[sel_slot, sc, hh, :, ls] / z

    for l0 in range(0, cw, LANES):
        select_column(l0)

    for c in range(n_chunks):
        a_ref[c] = jnp.dot(u_ref[...], ht_ref[use_slot, c], preferred_element_type=F32)
    for c in range(n_chunks):
        for il in range(rows_per_step):
            row = e * rows_per_step + il
            rs = slice(il * PEER_N_KEYS, (il + 1) * PEER_N_KEYS)
            pre = a_ref[c, rs, :]
            act = 0.5 * pre * (1.0 + lax.erf(pre * math.sqrt(0.5)))
            gate = jnp.zeros(pre.shape, BF16)
            for h2 in range(PEER_HEADS):
                cnt = _packed_rows(c1_ref[use_slot, c, h2, pl.ds(row, 1), :], PEER_N_KEYS)
                wgt = _packed_rows(e1_ref[use_slot, c, h2, pl.ds(row, 1), :], PEER_N_KEYS)
                gate = gate + jnp.where(r2_ref[use_slot, c, h2] < cnt, e2_ref[use_slot, c, h2],
                                        jnp.zeros((), BF16)) * wgt
            wt_ref[c, rs, :] = gate * act.astype(BF16)
        ft_ref[c] += jnp.dot(vt_ref[...], wt_ref[c], preferred_element_type=F32)

    @pl.when((e == n_e - 1) & (i > 0))
    def _finish():
        f = jnp.concatenate([ft_ref[c].T for c in range(n_chunks)], axis=0)
        out_ref[...] = _layer_norm(alpha * hp_ref[...] + f, g_ref[...], b_ref[...])


def _peer(h, wq_t, k1, k2, u_bf, vt_bf, ln_g, ln_b, alpha):
    t = h.shape[0]
    tm = min(512, t)
    eb = 1024
    n_exp = u_bf.shape[0]
    nt = t // tm
    n_e = n_exp // eb
    cw = min(TOKEN_CHUNK, tm)
    n_chunks = tm // cw
    assert n_e % (PEER_HEADS * n_chunks) == 0, "one selection part per expert-block step"

    def full(shape):
        return pl.BlockSpec(shape, lambda i, e: (0,) * len(shape))

    next_spec = pl.BlockSpec((tm, D_MODEL), lambda i, e: (jnp.minimum(i, nt - 1), 0))
    prev_spec = pl.BlockSpec((tm, D_MODEL), lambda i, e: (jnp.maximum(i - 1, 0), 0))
    meta32 = pltpu.VMEM((2, n_chunks, PEER_HEADS, PEER_N_KEYS, cw), F32)
    meta16 = pltpu.VMEM((2, n_chunks, PEER_HEADS, PEER_N_KEYS, cw), BF16)
    return pl.pallas_call(
        functools.partial(_peer_kernel, alpha=alpha, eb=eb, n_chunks=n_chunks),
        grid=(nt + 1, n_e),
        in_specs=[next_spec, prev_spec, full(wq_t.shape), full(k1.shape), full(k2.shape),
                  pl.BlockSpec((eb, D_MODEL), lambda i, e: (e, 0)),
                  pl.BlockSpec((D_MODEL, eb), lambda i, e: (0, e)),
                  full((1, D_MODEL)), full((1, D_MODEL))],
        out_specs=prev_spec,
        out_shape=jax.ShapeDtypeStruct((t, D_MODEL), F32),
        scratch_shapes=[pltpu.VMEM((2, n_chunks, D_MODEL, cw), BF16), meta32, meta32, meta16, meta16,
                        pltpu.VMEM((n_chunks, eb, cw), F32), pltpu.VMEM((n_chunks, eb, cw), BF16),
                        pltpu.VMEM((n_chunks, D_MODEL, cw), F32)],
        compiler_params=_cparams(("arbitrary", "arbitrary")), name="peer",
    )(h, h, wq_t, k1, k2, u_bf, vt_bf, ln_g, ln_b)


def _rope_tables(pos):
    half = HEAD_DIM // 2
    inv_freq = ROPE_THETA ** (-jnp.arange(half, dtype=F32) / half)
    ang = pos.astype(F32)[:, None] * inv_freq[None, :]
    cos, sin = jnp.cos(ang), jnp.sin(ang)
    cos_t = jnp.concatenate([cos, cos, cos, cos], axis=-1)
    sin_t = jnp.concatenate([-sin, sin, -sin, sin], axis=-1)
    return cos_t, sin_t


def _slab_rows_to_heads(slabs, lead):
    nl = len(lead)
    x = jnp.moveaxis(slabs, 0, nl + 1)
    return x.reshape(*lead, x.shape[nl], N_SLOTS, HEAD_DIM)


def kernel(x_prompt, x_sample, cache_kv_w128, cache_kv_w512, cache_kv_w2048, state_conv, w_in, conv_w, w_branch_attn, w_branch_conv, w_out, ln1_g, ln1_b, peer_w_q, peer_keys_1, peer_keys_2, peer_u, peer_v, ln2_g, ln2_b):
    depth = w_in.shape[0]
    assert depth == 1, "single-layer trunk"
    n, s, d = x_prompt.shape
    nb_, n_new, _ = x_sample.shape
    past_len = 8192
    alpha = (2.0 * depth) ** 0.25
    caches = (cache_kv_w128, cache_kv_w512, cache_kv_w2048)
    for (win, dil), c in zip(DIL_GROUPS, caches):
        assert c.shape[2] == win and s % (dil * BAND_BLOCK) == 0 and win // dil == SPAN and win <= s

    lyr = 0
    w_bf = w_in[lyr].astype(BF16)
    wpa = w_branch_attn[lyr].astype(BF16)
    wpc = w_branch_conv[lyr].astype(BF16)
    wo = w_out[lyr].astype(BF16)
    wq_t = peer_w_q[lyr].T.astype(BF16)
    k1 = peer_keys_1[lyr].astype(BF16)
    k2 = peer_keys_2[lyr].astype(BF16)
    u_bf = peer_u[lyr].astype(BF16)
    vt_bf = peer_v[lyr].T.astype(BF16)
    g1, b1 = ln1_g[lyr][None, :], ln1_b[lyr][None, :]
    g2, b2 = ln2_g[lyr][None, :], ln2_b[lyr][None, :]
    cw = conv_w[lyr]
    k_slab0 = N_GROUPS * GROUP_SLABS

    xp = x_prompt.reshape(n * s, d)
    cos_p, sin_p = _rope_tables(jnp.arange(s))
    qk_p, v_p, u_p, rest_p = _input_projections(xp, w_bf, cos_p, sin_p)
    outs, lses = [], []
    for gi, (win, dil) in enumerate(DIL_GROUPS):
        o, l = _prompt_attention(qk_p, v_p, n, s, gi, dil)
        outs.append(o)
        lses.append(l)
    h_p = _mixer_finish(xp, outs, lses, u_p, rest_p, u_p, u_p, s, cw, wpa, wpc, wo, g1, b1, alpha)
    y_p = _peer(h_p, wq_t, k1, k2, u_bf, vt_bf, g2, b2, alpha).reshape(n, s, d)

    qk_p4 = qk_p.reshape(qk_p.shape[0], n, s, LANES)
    v_p4 = v_p.reshape(v_p.shape[0], n, s, LANES)
    kv_p = []
    for gi, (win, dil) in enumerate(DIL_GROUPS):
        ks = k_slab0 + gi * GROUP_SLABS
        vs = gi * GROUP_SLABS
        k_tail = _slab_rows_to_heads(qk_p4[ks:ks + GROUP_SLABS, :, s - win:], (n,))
        v_tail = _slab_rows_to_heads(v_p4[vs:vs + GROUP_SLABS, :, s - win:], (n,))
        kv_p.append(jnp.stack([k_tail, v_tail], axis=2)[None])
    conv_state_p = u_p.reshape(n, s, d)[:, s - (CONV_WIDTH - 1):][None]

    xs = x_sample.reshape(nb_ * n_new, d)
    cos_s, sin_s = _rope_tables(jnp.tile(past_len + jnp.arange(n_new), nb_))
    qk_s, v_s, u_s, rest_s = _input_projections(xs, w_bf, cos_s, sin_s)
    qk_s4 = qk_s.reshape(qk_s.shape[0], nb_, n_new, LANES)
    v_s4 = v_s.reshape(v_s.shape[0], nb_, n_new, LANES)
    outs, lses, kv_s = [], [], []
    for gi, (win, dil) in enumerate(DIL_GROUPS):
        ks = k_slab0 + gi * GROUP_SLABS
        vs = gi * GROUP_SLABS
        k_new = _slab_rows_to_heads(qk_s4[ks:ks + GROUP_SLABS], (nb_,))
        v_new = _slab_rows_to_heads(v_s4[vs:vs + GROUP_SLABS], (nb_,))
        new_t = jnp.transpose(jnp.stack([k_new, v_new], axis=1), (0, 1, 3, 4, 2))
        new_t = jnp.pad(new_t, ((0, 0),) * 4 + ((LANES - n_new, 0),))
        o, l, cache_new = _sample_attention(qk_s, new_t, caches[gi][lyr], nb_, n_new, gi, win, dil)
        outs.append(o)
        lses.append(l)
        kv_s.append(cache_new[None])
    st = state_conv[lyr]
    zeros6 = jnp.zeros((nb_, n_new - 1, d), F32)
    hist1 = jnp.concatenate([st[:, 1:2], zeros6], axis=1).reshape(nb_ * n_new, d)
    hist2 = jnp.concatenate([st, zeros6[:, 1:]], axis=1).reshape(nb_ * n_new, d)
    h_s = _mixer_finish(xs, outs, lses, u_s, rest_s, hist1, hist2, n_new, cw, wpa, wpc, wo, g1, b1, alpha)
    y_s = _peer(h_s, wq_t, k1, k2, u_bf, vt_bf, g2, b2, alpha).reshape(nb_, n_new, d)

    u_ext_s = jnp.concatenate([st, u_s.reshape(nb_, n_new, d)], axis=1)
    conv_state_s = u_ext_s[:, u_ext_s.shape[1] - (CONV_WIDTH - 1):][None]

    return (y_p, y_s, kv_p[0], kv_p[1], kv_p[2], conv_state_p,
            kv_s[0], kv_s[1], kv_s[2], conv_state_s)
```

```python
import functools
import math

import numpy as np
import jax
import jax.numpy as jnp
from jax import lax
from jax.experimental import pallas as pl
from jax.experimental.pallas import tpu as pltpu

F32 = jnp.float32
BF16 = jnp.bfloat16

LANES = 128
MXU_WIDTH = 256
D_MODEL = 1024
N_GROUPS = 3
N_SLOTS = 8
HEAD_DIM = 64
GROUP_W = N_SLOTS * HEAD_DIM
GROUP_SLABS = GROUP_W // LANES
ATT_W = N_GROUPS * GROUP_W
DIL_GROUPS = ((128, 1), (512, 4), (2048, 16))
SPAN = 128
BAND_BLOCK = 128
ROPE_THETA = 10000.0
CONV_WIDTH = 3
PEER_HEADS = 8
PEER_N_KEYS = 128
PEER_D_KEY = 256
PEER_TOPK = 16
LN_EPS = 1e-5
NEG_INF = -1e30
TOKEN_CHUNK = 256
RESIDUE_UNROLL = 4

COL_TILE = 512
_QK_TILE0, _V_TILE0, _CIN_TILE0, _CB_TILE0, _CC_TILE0, _GA_TILE0 = 0, 6, 9, 11, 13, 15

VMEM_LIMIT = 56 * 1024 * 1024


def _cparams(sem):
    return pltpu.CompilerParams(dimension_semantics=sem, vmem_limit_bytes=VMEM_LIMIT)


def _proj_rope_kernel(x_ref, w_ref, cos_ref, sin_ref, o_ref, *, n_q_tiles):
    j = pl.program_id(1)
    xb = x_ref[...].astype(BF16)
    cos = cos_ref[...]
    sin = sin_ref[...]
    lane = lax.broadcasted_iota(jnp.int32, cos.shape, 1)
    first_half = (lane & (HEAD_DIM // 2)) == 0
    scale = jnp.where(j < n_q_tiles, HEAD_DIM ** -0.5, 1.0).astype(F32)
    for half in range(COL_TILE // MXU_WIDTH):
        acc = jnp.dot(xb, w_ref[:, half * MXU_WIDTH:(half + 1) * MXU_WIDTH], preferred_element_type=F32)
        for c in range(MXU_WIDTH // LANES):
            a = acc[:, c * LANES:(c + 1) * LANES]
            partner = jnp.where(first_half, pltpu.roll(a, LANES - HEAD_DIM // 2, 1), pltpu.roll(a, HEAD_DIM // 2, 1))
            o_ref[half * (MXU_WIDTH // LANES) + c] = (a * cos + partner * sin) * scale


def _proj_slab_kernel(x_ref, w_ref, o_ref):
    acc = jnp.dot(x_ref[...].astype(BF16), w_ref[...], preferred_element_type=F32)
    for c in range(COL_TILE // LANES):
        o_ref[c] = acc[:, c * LANES:(c + 1) * LANES]


def _proj_kernel(x_ref, w_ref, o_ref):
    o_ref[...] = jnp.dot(x_ref[...].astype(BF16), w_ref[...], preferred_element_type=F32)


def _proj_product_kernel(x_ref, w1_ref, w2_ref, o_ref):
    xb = x_ref[...].astype(BF16)
    o_ref[...] = (jnp.dot(xb, w1_ref[...], preferred_element_type=F32)
                  * jnp.dot(xb, w2_ref[...], preferred_element_type=F32))


def _input_projections(x, w_bf, cos_t, sin_t):
    t = x.shape[0]
    tm = min(1024, t)
    nt = t // tm
    n_pos_tiles = cos_t.shape[0] // tm
    x_spec = pl.BlockSpec((tm, D_MODEL), lambda i, j: (i, 0))

    def w_spec(col_fn):
        return pl.BlockSpec((D_MODEL, COL_TILE), lambda i, j: (0, col_fn(j)))

    o_spec = pl.BlockSpec((tm, COL_TILE), lambda i, j: (i, j))
    slab_spec = pl.BlockSpec((COL_TILE // LANES, tm, LANES), lambda i, j: (j, i, 0))
    pos_spec = pl.BlockSpec((tm, LANES), lambda i, j: (i % n_pos_tiles, 0))
    sem = ("parallel", "arbitrary")

    n_qk = 2 * ATT_W // COL_TILE
    qk = pl.pallas_call(
        functools.partial(_proj_rope_kernel, n_q_tiles=ATT_W // COL_TILE),
        grid=(nt, n_qk),
        in_specs=[x_spec, w_spec(lambda j: j + _QK_TILE0), pos_spec, pos_spec],
        out_specs=slab_spec,
        out_shape=jax.ShapeDtypeStruct((2 * ATT_W // LANES, t, LANES), F32),
        compiler_params=_cparams(sem), name="proj_qk_rope",
    )(x, w_bf, cos_t, sin_t)

    v = pl.pallas_call(
        _proj_slab_kernel, grid=(nt, ATT_W // COL_TILE),
        in_specs=[x_spec, w_spec(lambda j: j + _V_TILE0)],
        out_specs=slab_spec, out_shape=jax.ShapeDtypeStruct((ATT_W // LANES, t, LANES), F32),
        compiler_params=_cparams(sem), name="proj_v",
    )(x, w_bf)

    u = pl.pallas_call(
        _proj_product_kernel, grid=(nt, D_MODEL // COL_TILE),
        in_specs=[x_spec, w_spec(lambda j: j + _CIN_TILE0), w_spec(lambda j: j + _CC_TILE0)],
        out_specs=o_spec, out_shape=jax.ShapeDtypeStruct((t, D_MODEL), F32),
        compiler_params=_cparams(sem), name="proj_conv_u",
    )(x, w_bf, w_bf)

    rest = pl.pallas_call(
        _proj_kernel, grid=(nt, 3 * D_MODEL // COL_TILE),
        in_specs=[x_spec, w_spec(lambda j: jnp.where(j < 2, j + _CB_TILE0, j - 2 + _GA_TILE0))],
        out_specs=o_spec, out_shape=jax.ShapeDtypeStruct((t, 3 * D_MODEL), F32),
        compiler_params=_cparams(sem), name="proj_cb_gates",
    )(x, w_bf)
    return qk, v, u, rest


def _band_attn_kernel(q_ref, kp_ref, kc_ref, vp_ref, vc_ref, o_ref, l_ref, *, dil, nbb):
    b = pl.program_id(2)
    qi = lax.broadcasted_iota(jnp.int32, (BAND_BLOCK, 2 * BAND_BLOCK), 0)
    ki = lax.broadcasted_iota(jnp.int32, (BAND_BLOCK, 2 * BAND_BLOCK), 1)
    dist = BAND_BLOCK + qi - ki
    in_band = (dist >= 0) & (dist <= SPAN)
    first_mask = in_band & (ki >= jnp.where(b > 0, 0, BAND_BLOCK))

    def rows(ref, start):
        if dil == 1:
            return ref[pl.ds(start, BAND_BLOCK), :]
        return ref[pl.ds(start, BAND_BLOCK, stride=dil), :]

    def residue(r):
        for bi in range(nbb):
            base = bi * BAND_BLOCK * dil + r
            q = rows(q_ref, base).astype(BF16)
            if bi == 0:
                kp, vp = rows(kp_ref, r), rows(vp_ref, r)
            else:
                kp, vp = rows(kc_ref, base - BAND_BLOCK * dil), rows(vc_ref, base - BAND_BLOCK * dil)
            k = jnp.concatenate([kp, rows(kc_ref, base)], axis=0).astype(BF16)
            v = jnp.concatenate([vp, rows(vc_ref, base)], axis=0).astype(BF16)
            mask = first_mask if bi == 0 else in_band
            outs, lses = [], []
            for hl in range(LANES // HEAD_DIM):
                sl = slice(hl * HEAD_DIM, (hl + 1) * HEAD_DIM)
                s = lax.dot_general(q[:, sl], k[:, sl], (((1,), (1,)), ((), ())), preferred_element_type=F32)
                s = jnp.where(mask, s, NEG_INF)
                mx = jnp.max(s, axis=-1, keepdims=True)
                p = jnp.exp(s - mx)
                den = jnp.sum(p, axis=-1, keepdims=True)
                outs.append(jnp.dot(p.astype(BF16), v[:, sl], preferred_element_type=F32) / den)
                lses.append(jnp.broadcast_to(mx + jnp.log(den), (BAND_BLOCK, HEAD_DIM)))
            o = jnp.concatenate(outs, axis=1)
            l = jnp.concatenate(lses, axis=1)
            if dil == 1:
                o_ref[pl.ds(base, BAND_BLOCK), :] = o
                l_ref[pl.ds(base, BAND_BLOCK), :] = l
            else:
                o_ref[pl.ds(base, BAND_BLOCK, stride=dil), :] = o
                l_ref[pl.ds(base, BAND_BLOCK, stride=dil), :] = l

    if dil <= RESIDUE_UNROLL:
        for r in range(dil):
            residue(r)
    else:
        def residue_group(rg, carry):
            for ru in range(RESIDUE_UNROLL):
                residue(rg * RESIDUE_UNROLL + ru)
            return carry

        lax.fori_loop(0, dil // RESIDUE_UNROLL, residue_group, 0)


def _prompt_attention(qk, v, n, s, gi, dil):
    band_rows = BAND_BLOCK * dil
    nbb = max(512 // band_rows, 1)
    rows = band_rows * nbb
    qk4 = qk.reshape(qk.shape[0], n, s, LANES)
    v4 = v.reshape(v.shape[0], n, s, LANES)
    cur = (None, None, rows, LANES)
    prv = (None, None, band_rows, LANES)
    q0 = gi * GROUP_SLABS
    k0 = (N_GROUPS + gi) * GROUP_SLABS
    v0 = gi * GROUP_SLABS

    def prev(b):
        return jnp.maximum(b * nbb - 1, 0)

    in_specs = [
        pl.BlockSpec(cur, lambda a, c, b: (q0 + c, a, b, 0)),
        pl.BlockSpec(prv, lambda a, c, b: (k0 + c, a, prev(b), 0)),
        pl.BlockSpec(cur, lambda a, c, b: (k0 + c, a, b, 0)),
        pl.BlockSpec(prv, lambda a, c, b: (v0 + c, a, prev(b), 0)),
        pl.BlockSpec(cur, lambda a, c, b: (v0 + c, a, b, 0)),
    ]
    o_spec = pl.BlockSpec(cur, lambda a, c, b: (c, a, b, 0))
    out, lse = pl.pallas_call(
        functools.partial(_band_attn_kernel, dil=dil, nbb=nbb), grid=(n, GROUP_SLABS, s // rows),
        in_specs=in_specs, out_specs=[o_spec, o_spec],
        out_shape=[jax.ShapeDtypeStruct((GROUP_SLABS, n, s, LANES), F32)] * 2,
        compiler_params=_cparams(("parallel", "parallel", "arbitrary")), name=f"band_attn_d{dil}",
    )(qk4, qk4, qk4, v4, v4)
    return out.reshape(GROUP_SLABS, n * s, LANES), lse.reshape(GROUP_SLABS, n * s, LANES)


def _cached_attn_kernel(q_ref, c_ref, n_ref, bias_ref, co_ref, o_ref, l_ref, *, win, bb, n_new):
    bias = bias_ref[...]
    lane = lax.broadcasted_iota(jnp.int32, (HEAD_DIM, LANES), 1)
    is_new = lane >= LANES - n_new

    def body(b, carry):
        for h in range(N_SLOTS):
            hs, hl = h // 2, h % 2
            sl = slice(hl * HEAD_DIM, (hl + 1) * HEAD_DIM)
            q = q_ref[hs, b][:, sl]
            kt = jnp.concatenate([c_ref[b, 0, h], n_ref[b, 0, h]], axis=1)
            vt = jnp.concatenate([c_ref[b, 1, h], n_ref[b, 1, h]], axis=1)
            s = jnp.dot(q, kt, preferred_element_type=F32) + bias
            mx = jnp.max(s, axis=-1, keepdims=True)
            p = jnp.exp(s - mx)
            den = jnp.sum(p, axis=-1, keepdims=True)
            o = lax.dot_general(p, vt, (((1,), (1,)), ((), ())), preferred_element_type=F32) / den
            o_ref[hs, b, :, sl] = o
            l_ref[hs, b, :, sl] = jnp.broadcast_to(mx + jnp.log(den), (n_new, HEAD_DIM))
            for kv in range(2):
                rolled = pltpu.roll(c_ref[b, kv, h], win - n_new, 1)
                if win > LANES:
                    co_ref[b, kv, h, :, :win - LANES] = rolled[:, :win - LANES]
                co_ref[b, kv, h, :, win - LANES:] = jnp.where(is_new, n_ref[b, kv, h], rolled[:, win - LANES:])
        return carry

    lax.fori_loop(0, bb, body, 0)


def _cache_bias(win, dil, n_new):
    t = np.arange(n_new)[:, None]
    j = np.arange(win)[None, :]
    dist = win + t - j
    ok_c = (dist % dil == 0) & (dist <= dil * SPAN)
    l = np.arange(LANES)[None, :]
    tn = l - (LANES - n_new)
    ok_n = (tn >= 0) & (tn <= t) & ((t - tn) % dil == 0)
    ok = np.concatenate([ok_c, ok_n], axis=1)
    return jnp.asarray(np.where(ok, 0.0, NEG_INF), dtype=F32)


def _sample_attention(qk, new_t, cache, nb_, n_new, gi, win, dil):
    bb = max(1, min(8, 2048 // win))
    while nb_ % bb:
        bb //= 2
    qk4 = qk.reshape(qk.shape[0], nb_, n_new, LANES)
    cache_t = jnp.transpose(cache, (0, 2, 3, 4, 1))
    bias = _cache_bias(win, dil, n_new)
    c_blk = (bb, 2, N_SLOTS, HEAD_DIM, win)
    n_blk = (bb, 2, N_SLOTS, HEAD_DIM, LANES)
    q_blk = (GROUP_SLABS, bb, n_new, LANES)
    in_specs = [
        pl.BlockSpec(q_blk, lambda i: (gi, i, 0, 0)),
        pl.BlockSpec(c_blk, lambda i: (i, 0, 0, 0, 0)),
        pl.BlockSpec(n_blk, lambda i: (i, 0, 0, 0, 0)),
        pl.BlockSpec(bias.shape, lambda i: (0, 0)),
    ]
    o_spec = pl.BlockSpec(q_blk, lambda i: (0, i, 0, 0))
    cache_o, out, lse = pl.pallas_call(
        functools.partial(_cached_attn_kernel, win=win, bb=bb, n_new=n_new),
        grid=(nb_ // bb,),
        in_specs=in_specs,
        out_specs=[pl.BlockSpec(c_blk, lambda i: (i, 0, 0, 0, 0)), o_spec, o_spec],
        out_shape=[jax.ShapeDtypeStruct(cache_t.shape, F32),
                   jax.ShapeDtypeStruct((GROUP_SLABS, nb_, n_new, LANES), F32),
                   jax.ShapeDtypeStruct((GROUP_SLABS, nb_, n_new, LANES), F32)],
        compiler_params=_cparams(("parallel",)), name=f"cached_attn_d{dil}",
    )(qk4, cache_t, new_t, bias)
    new_cache = jnp.transpose(cache_o, (0, 4, 1, 2, 3))
    return (out.reshape(GROUP_SLABS, nb_ * n_new, LANES), lse.reshape(GROUP_SLABS, nb_ * n_new, LANES), new_cache)


def _layer_norm(x, g, b):
    mu = jnp.mean(x, axis=-1, keepdims=True)
    xc = x - mu
    var = jnp.mean(xc * xc, axis=-1, keepdims=True)
    return xc * lax.rsqrt(var + LN_EPS) * g + b


def _mix_kernel(x_ref, o0_ref, o1_ref, o2_ref, l0_ref, l1_ref, l2_ref, u_ref, h1_ref, h2_ref,
                cb_ref, ga_ref, gc_ref, cw_ref, wpa_ref, wpc_ref, wo_ref, g_ref, b_ref, out_ref,
                *, seg_len, alpha):
    tm = x_ref.shape[0]
    parts = []
    for c in range(GROUP_SLABS):
        l0, l1, l2 = l0_ref[c], l1_ref[c], l2_ref[c]
        lm = jnp.maximum(jnp.maximum(l0, l1), l2)
        w0, w1, w2 = jnp.exp(l0 - lm), jnp.exp(l1 - lm), jnp.exp(l2 - lm)
        parts.append((w0 * o0_ref[c] + w1 * o1_ref[c] + w2 * o2_ref[c]) / (w0 + w1 + w2))
    y_att = jnp.concatenate(parts, axis=1)
    u = u_ref[...]
    row = lax.broadcasted_iota(jnp.int32, u.shape, 0)
    if seg_len >= tm:
        keep = jnp.where((pl.program_id(0) * tm) % seg_len == 0, 0.0, 1.0).astype(F32)
        hrow1 = h1_ref[7:8, :] * keep
        hrow2 = h1_ref[6:7, :] * keep
        prev1 = jnp.where(row == 0, hrow1, pltpu.roll(u, 1, 0))
        prev2 = jnp.where(row == 0, hrow2, jnp.where(row == 1, hrow1, pltpu.roll(u, 2, 0)))
    else:
        rs = row & (seg_len - 1)
        prev1 = jnp.where(rs == 0, h1_ref[...], pltpu.roll(u, 1, 0))
        prev2 = jnp.where(rs < 2, h2_ref[...], pltpu.roll(u, 2, 0))
    cw = cw_ref[...]
    z_conv = prev2 * cw[0:1, :] + prev1 * cw[1:2, :] + u * cw[2:3, :]
    cv = cb_ref[...] * z_conv
    br_a = jnp.dot(y_att.astype(BF16), wpa_ref[...], preferred_element_type=F32)
    br_c = jnp.dot(cv.astype(BF16), wpc_ref[...], preferred_element_type=F32)
    merged = jax.nn.sigmoid(ga_ref[...]) * br_a + jax.nn.sigmoid(gc_ref[...]) * br_c
    mix = jnp.dot(merged.astype(BF16), wo_ref[...], preferred_element_type=F32)
    out_ref[...] = _layer_norm(alpha * x_ref[...] + mix, g_ref[...], b_ref[...])


def _mixer_finish(x, outs, lses, u, rest, hist1, hist2, seg_len, conv_w, wpa, wpc, wo, ln_g, ln_b, alpha):
    t = x.shape[0]
    tm = min(512, t)
    nt = t // tm
    row_spec = pl.BlockSpec((tm, D_MODEL), lambda i: (i, 0))
    att_spec = pl.BlockSpec((GROUP_SLABS, tm, LANES), lambda i: (0, i, 0))
    if seg_len >= tm:
        hb = tm // 8
        h1_spec = pl.BlockSpec((8, D_MODEL), lambda i: (jnp.maximum(i * hb - 1, 0), 0))
        h2_spec = h1_spec
    else:
        h1_spec = h2_spec = row_spec

    def full(shape):
        return pl.BlockSpec(shape, lambda i: (0,) * len(shape))

    in_specs = ([row_spec] + [att_spec] * 6 + [row_spec, h1_spec, h2_spec]
                + [pl.BlockSpec((tm, D_MODEL), lambda i, c=c: (i, c)) for c in range(3)]
                + [full((CONV_WIDTH, D_MODEL)), full((GROUP_W, D_MODEL)), full((D_MODEL, D_MODEL)),
                   full((D_MODEL, D_MODEL)), full((1, D_MODEL)), full((1, D_MODEL))])
    return pl.pallas_call(
        functools.partial(_mix_kernel, seg_len=seg_len, alpha=alpha),
        grid=(nt,), in_specs=in_specs, out_specs=row_spec,
        out_shape=jax.ShapeDtypeStruct((t, D_MODEL), F32),
        compiler_params=_cparams(("parallel",)), name="mixer_finish",
    )(x, *outs, *lses, u, hist1, hist2, rest, rest, rest, conv_w, wpa, wpc, wo, ln_g, ln_b)


def _ranked_top16(s, exact_ties):
    n_keys = s.shape[0]
    rowf = lax.broadcasted_iota(jnp.int32, s.shape, 0).astype(F32)
    rank = jnp.full(s.shape, float(PEER_N_KEYS - 1), F32)
    tops = []
    for k in range(PEER_TOPK):
        m = jnp.max(s, axis=0, keepdims=True)
        if exact_ties:
            pos = jnp.min(jnp.where(s == m, rowf, float(n_keys)), axis=0, keepdims=True)
            hit = rowf == pos
        else:
            hit = s == m
        s = jnp.where(hit, -jnp.inf, s)
        rank = jnp.where(hit, float(k), rank)
        tops.append(m)
    return rank, tops


def _not_16(flags):
    total = jnp.sum(flags, axis=0, keepdims=True)
    return jnp.where(total == float(PEER_TOPK), 0.0, 1.0)


def _rows_to_tile(rows):
    tmw = rows[0].shape[1]
    ri = lax.broadcasted_iota(jnp.int32, (8, tmw), 0)
    out = jnp.zeros((8, tmw), F32)
    for a, rw in enumerate(rows):
        out = jnp.where(ri == a, rw, out)
    return out


def _select_counts(t1, t2, exact_ties):
    tmw = t1[0].shape[1]
    ri = lax.broadcasted_iota(jnp.int32, (8, tmw), 0)
    rif = ri.astype(F32)
    ninf = -jnp.inf
    v1lo, v1hi = _rows_to_tile(t1[:8]), _rows_to_tile(t1[8:])
    v2lo, v2hi = _rows_to_tile(t2[:8]), _rows_to_tile(t2[8:])
    all_rows = ri >= 0
    tiles = [
        (t1[0] + v2lo, rif, all_rows),
        (t1[0] + v2hi, 8.0 + rif, all_rows),
        (t1[1] + v2lo, 16.0 + rif, all_rows),
        (v1hi + t2[0], (8.0 + rif) * 16.0, all_rows),
        (v1lo + t2[0], rif * 16.0, ri >= 2),
        (v1lo + t2[1], rif * 16.0 + 1.0, ri >= 2),
        (t1[2] + v2lo, 32.0 + rif, (ri >= 2) & (ri <= 4)),
        (t1[3] + v2lo, 48.0 + rif, (ri >= 2) & (ri <= 3)),
        (t1[4] + v2lo, 64.0 + rif, ri == 2),
    ]
    vals = [jnp.where(ok, sm, ninf) for sm, _, ok in tiles]
    poss = [jnp.where(ok, ps, -1.0) for _, ps, ok in tiles]
    hits = [jnp.zeros((8, tmw), F32) for _ in vals]
    top = t1[0] + t2[0]
    z = jnp.zeros((1, tmw), F32)
    big = 1e9
    for _ in range(PEER_TOPK):
        m8 = vals[0]
        for vv in vals[1:]:
            m8 = jnp.maximum(m8, vv)
        m = jnp.max(m8, axis=0, keepdims=True)
        if exact_ties:
            p8 = jnp.where(vals[0] == m, poss[0], big)
            for vv, pp in zip(vals[1:], poss[1:]):
                p8 = jnp.minimum(p8, jnp.where(vv == m, pp, big))
            pm = jnp.min(p8, axis=0, keepdims=True)
        for c in range(len(vals)):
            hit = (poss[c] == pm) if exact_ties else (vals[c] == m)
            vals[c] = jnp.where(hit, ninf, vals[c])
            hits[c] = jnp.where(hit, 1.0, hits[c])
        z = z + jnp.exp(m - top)

    def colsum(x):
        return jnp.sum(x, axis=0, keepdims=True)

    n_picked = hits[0]
    for hh in hits[1:]:
        n_picked = n_picked + hh
    not16 = _not_16(n_picked)
    lo = hits[4] + hits[5]
    counts = [colsum(hits[0]) + colsum(hits[1]), colsum(hits[2]),
              lo[2:3] + colsum(hits[6]), lo[3:4] + colsum(hits[7]), lo[4:5] + colsum(hits[8]),
              lo[5:6], lo[6:7], lo[7:8]]
    counts += [hits[3][a:a + 1] for a in range(8)]
    return counts, z, not16


def _packed_rows(row, n_rows):
    tile = jnp.broadcast_to(row, (16, row.shape[1])).astype(BF16)
    return jnp.concatenate([tile] * (n_rows // 16), axis=0)


def _peer_kernel(hn_ref, hp_ref, wq_ref, k1_ref, k2_ref, u_ref, vt_ref, g_ref, b_ref, out_ref,
                 ht_ref, c1_ref, e1_ref, r2_ref, e2_ref, a_ref, wt_ref, ft_ref, *, alpha, eb, n_chunks):
    i = pl.program_id(0)
    e = pl.program_id(1)
    n_e = pl.num_programs(1)
    half = PEER_D_KEY // 2
    rows_per_step = eb // PEER_N_KEYS
    cw = ht_ref.shape[-1]
    sel_slot = i % 2
    use_slot = (i + 1) % 2

    @pl.when((i == 0) & (e == 0))
    def _init():
        ht_ref[1] = jnp.zeros(ht_ref.shape[1:], BF16)
        c1_ref[1] = jnp.zeros(c1_ref.shape[1:], F32)
        e1_ref[1] = jnp.zeros(e1_ref.shape[1:], F32)
        r2_ref[1] = jnp.zeros(r2_ref.shape[1:], BF16)
        e2_ref[1] = jnp.zeros(e2_ref.shape[1:], BF16)

    @pl.when(e == 0)
    def _start_tile():
        ht = hn_ref[...].T.astype(BF16)
        for c in range(n_chunks):
            ht_ref[sel_slot, c] = ht[:, c * cw:(c + 1) * cw]
        ft_ref[...] = jnp.zeros(ft_ref.shape, F32)

    part = e % (PEER_HEADS * n_chunks)
    hh = part // n_chunks
    sc = part % n_chunks
    off = pl.multiple_of(hh * PEER_D_KEY, PEER_D_KEY)

    def select_part(exact_ties):
        qy = jnp.dot(wq_ref[pl.ds(off, PEER_D_KEY), :], ht_ref[sel_slot, sc], preferred_element_type=F32)
        s1 = jnp.dot(k1_ref[...], qy[:half].astype(BF16), preferred_element_type=F32)
        s2 = jnp.dot(k2_ref[...], qy[half:].astype(BF16), preferred_element_type=F32)
        suspect = jnp.zeros((1, LANES), F32)
        for l0 in range(0, cw, LANES):
            ls = slice(l0, l0 + LANES)
            s1c, s2c = s1[:, ls], s2[:, ls]
            e1_ref[sel_slot, sc, hh, :, ls] = jnp.exp(s1c - jnp.max(s1c, axis=0, keepdims=True))
            e2_ref[sel_slot, sc, hh, :, ls] = jnp.exp(s2c - jnp.max(s2c, axis=0, keepdims=True)).astype(BF16)
            r2, t2 = _ranked_top16(s2c, exact_ties)
            r2_ref[sel_slot, sc, hh, :, ls] = r2.astype(BF16)
            r1, t1 = _ranked_top16(s1c, exact_ties)
            counts, z, not16 = _select_counts(t1, t2, exact_ties)
            c1 = jnp.zeros(s1c.shape, F32)
            for a in range(PEER_TOPK):
                c1 = jnp.where(r1 == float(a), counts[a], c1)
            c1_ref[sel_slot, sc, hh, :, ls] = c1
            e1_ref[sel_slot, sc, hh, :, ls] = e1_ref[sel_slot, sc, hh, :, ls] / z
            if not exact_ties:
                ranked = float(PEER_N_KEYS - 1)
                suspect = jnp.maximum(suspect, not16)
                suspect = jnp.maximum(suspect, _not_16(jnp.where(r1 < ranked, 1.0, 0.0)))
                suspect = jnp.maximum(suspect, _not_16(jnp.where(r2 < ranked, 1.0, 0.0)))
        return suspect

    any_suspect = jnp.max(select_part(exact_ties=False))

    for c in range(n_chunks):
        a_ref[c] = jnp.dot(u_ref[...], ht_ref[use_slot, c], preferred_element_type=F32)
    for c in range(n_chunks):
        for il in range(rows_per_step):
            row = e * rows_per_step + il
            rs = slice(il * PEER_N_KEYS, (il + 1) * PEER_N_KEYS)
            pre = a_ref[c, rs, :]
            act = 0.5 * pre * (1.0 + lax.erf(pre * math.sqrt(0.5)))
            gate = jnp.zeros(pre.shape, BF16)
            for h2 in range(PEER_HEADS):
                cnt = _packed_rows(c1_ref[use_slot, c, h2, pl.ds(row, 1), :], PEER_N_KEYS)
                wgt = _packed_rows(e1_ref[use_slot, c, h2, pl.ds(row, 1), :], PEER_N_KEYS)
                gate = gate + jnp.where(r2_ref[use_slot, c, h2] < cnt, e2_ref[use_slot, c, h2],
                                        jnp.zeros((), BF16)) * wgt
            wt_ref[c, rs, :] = gate * act.astype(BF16)
        ft_ref[c] += jnp.dot(vt_ref[...], wt_ref[c], preferred_element_type=F32)

    @pl.when(any_suspect > 0.0)
    def _redo_with_ties():
        select_part(exact_ties=True)

    @pl.when((e == n_e - 1) & (i > 0))
    def _finish():
        f = jnp.concatenate([ft_ref[c].T for c in range(n_chunks)], axis=0)
        out_ref[...] = _layer_norm(alpha * hp_ref[...] + f, g_ref[...], b_ref[...])


def _peer(h, wq_t, k1, k2, u_bf, vt_bf, ln_g, ln_b, alpha):
    t = h.shape[0]
    tm = min(512, t)
    eb = 1024
    n_exp = u_bf.shape[0]
    nt = t // tm
    n_e = n_exp // eb
    cw = min(TOKEN_CHUNK, tm)
    n_chunks = tm // cw
    assert n_e % (PEER_HEADS * n_chunks) == 0, "one selection part per expert-block step"

    def full(shape):
        return pl.BlockSpec(shape, lambda i, e: (0,) * len(shape))

    next_spec = pl.BlockSpec((tm, D_MODEL), lambda i, e: (jnp.minimum(i, nt - 1), 0))
    prev_spec = pl.BlockSpec((tm, D_MODEL), lambda i, e: (jnp.maximum(i - 1, 0), 0))
    meta32 = pltpu.VMEM((2, n_chunks, PEER_HEADS, PEER_N_KEYS, cw), F32)
    meta16 = pltpu.VMEM((2, n_chunks, PEER_HEADS, PEER_N_KEYS, cw), BF16)
    return pl.pallas_call(
        functools.partial(_peer_kernel, alpha=alpha, eb=eb, n_chunks=n_chunks),
        grid=(nt + 1, n_e),
        in_specs=[next_spec, prev_spec, full(wq_t.shape), full(k1.shape), full(k2.shape),
                  pl.BlockSpec((eb, D_MODEL), lambda i, e: (e, 0)),
                  pl.BlockSpec((D_MODEL, eb), lambda i, e: (0, e)),
                  full((1, D_MODEL)), full((1, D_MODEL))],
        out_specs=prev_spec,
        out_shape=jax.ShapeDtypeStruct((t, D_MODEL), F32),
        scratch_shapes=[pltpu.VMEM((2, n_chunks, D_MODEL, cw), BF16), meta32, meta32, meta16, meta16,
                        pltpu.VMEM((n_chunks, eb, cw), F32), pltpu.VMEM((n_chunks, eb, cw), BF16),
                        pltpu.VMEM((n_chunks, D_MODEL, cw), F32)],
        compiler_params=_cparams(("arbitrary", "arbitrary")), name="peer",
    )(h, h, wq_t, k1, k2, u_bf, vt_bf, ln_g, ln_b)


def _rope_tables(pos):
    half = HEAD_DIM // 2
    inv_freq = ROPE_THETA ** (-jnp.arange(half, dtype=F32) / half)
    ang = pos.astype(F32)[:, None] * inv_freq[None, :]
    cos, sin = jnp.cos(ang), jnp.sin(ang)
    cos_t = jnp.concatenate([cos, cos, cos, cos], axis=-1)
    sin_t = jnp.concatenate([-sin, sin, -sin, sin], axis=-1)
    return cos_t, sin_t


def _slab_rows_to_heads(slabs, lead):
    nl = len(lead)
    x = jnp.moveaxis(slabs, 0, nl + 1)
    return x.reshape(*lead, x.shape[nl], N_SLOTS, HEAD_DIM)


def kernel(x_prompt, x_sample, cache_kv_w128, cache_kv_w512, cache_kv_w2048, state_conv, w_in, conv_w, w_branch_attn, w_branch_conv, w_out, ln1_g, ln1_b, peer_w_q, peer_keys_1, peer_keys_2, peer_u, peer_v, ln2_g, ln2_b):
    depth = w_in.shape[0]
    assert depth == 1, "single-layer trunk"
    n, s, d = x_prompt.shape
    nb_, n_new, _ = x_sample.shape
    past_len = 8192
    alpha = (2.0 * depth) ** 0.25
    caches = (cache_kv_w128, cache_kv_w512, cache_kv_w2048)
    for (win, dil), c in zip(DIL_GROUPS, caches):
        assert c.shape[2] == win and s % (dil * BAND_BLOCK) == 0 and win // dil == SPAN and win <= s

    lyr = 0
    w_bf = w_in[lyr].astype(BF16)
    wpa = w_branch_attn[lyr].astype(BF16)
    wpc = w_branch_conv[lyr].astype(BF16)
    wo = w_out[lyr].astype(BF16)
    wq_t = peer_w_q[lyr].T.astype(BF16)
    k1 = peer_keys_1[lyr].astype(BF16)
    k2 = peer_keys_2[lyr].astype(BF16)
    u_bf = peer_u[lyr].astype(BF16)
    vt_bf = peer_v[lyr].T.astype(BF16)
    g1, b1 = ln1_g[lyr][None, :], ln1_b[lyr][None, :]
    g2, b2 = ln2_g[lyr][None, :], ln2_b[lyr][None, :]
    cw = conv_w[lyr]
    k_slab0 = N_GROUPS * GROUP_SLABS

    xp = x_prompt.reshape(n * s, d)
    cos_p, sin_p = _rope_tables(jnp.arange(s))
    qk_p, v_p, u_p, rest_p = _input_projections(xp, w_bf, cos_p, sin_p)
    outs, lses = [], []
    for gi, (win, dil) in enumerate(DIL_GROUPS):
        o, l = _prompt_attention(qk_p, v_p, n, s, gi, dil)
        outs.append(o)
        lses.append(l)
    h_p = _mixer_finish(xp, outs, lses, u_p, rest_p, u_p, u_p, s, cw, wpa, wpc, wo, g1, b1, alpha)
    y_p = _peer(h_p, wq_t, k1, k2, u_bf, vt_bf, g2, b2, alpha).reshape(n, s, d)

    qk_p4 = qk_p.reshape(qk_p.shape[0], n, s, LANES)
    v_p4 = v_p.reshape(v_p.shape[0], n, s, LANES)
    kv_p = []
    for gi, (win, dil) in enumerate(DIL_GROUPS):
        ks = k_slab0 + gi * GROUP_SLABS
        vs = gi * GROUP_SLABS
        k_tail = _slab_rows_to_heads(qk_p4[ks:ks + GROUP_SLABS, :, s - win:], (n,))
        v_tail = _slab_rows_to_heads(v_p4[vs:vs + GROUP_SLABS, :, s - win:], (n,))
        kv_p.append(jnp.stack([k_tail, v_tail], axis=2)[None])
    conv_state_p = u_p.reshape(n, s, d)[:, s - (CONV_WIDTH - 1):][None]

    xs = x_sample.reshape(nb_ * n_new, d)
    cos_s, sin_s = _rope_tables(jnp.tile(past_len + jnp.arange(n_new), nb_))
    qk_s, v_s, u_s, rest_s = _input_projections(xs, w_bf, cos_s, sin_s)
    qk_s4 = qk_s.reshape(qk_s.shape[0], nb_, n_new, LANES)
    v_s4 = v_s.reshape(v_s.shape[0], nb_, n_new, LANES)
    outs, lses, kv_s = [], [], []
    for gi, (win, dil) in enumerate(DIL_GROUPS):
        ks = k_slab0 + gi * GROUP_SLABS
        vs = gi * GROUP_SLABS
        k_new = _slab_rows_to_heads(qk_s4[ks:ks + GROUP_SLABS], (nb_,))
        v_new = _slab_rows_to_heads(v_s4[vs:vs + GROUP_SLABS], (nb_,))
        new_t = jnp.transpose(jnp.stack([k_new, v_new], axis=1), (0, 1, 3, 4, 2))
        new_t = jnp.pad(new_t, ((0, 0),) * 4 + ((LANES - n_new, 0),))
        o, l, cache_new = _sample_attention(qk_s, new_t, caches[gi][lyr], nb_, n_new, gi, win, dil)
        outs.append(o)
        lses.append(l)
        kv_s.append(cache_new[None])
    st = state_conv[lyr]
    zeros6 = jnp.zeros((nb_, n_new - 1, d), F32)
    hist1 = jnp.concatenate([st[:, 1:2], zeros6], axis=1).reshape(nb_ * n_new, d)
    hist2 = jnp.concatenate([st, zeros6[:, 1:]], axis=1).reshape(nb_ * n_new, d)
    h_s = _mixer_finish(xs, outs, lses, u_s, rest_s, hist1, hist2, n_new, cw, wpa, wpc, wo, g1, b1, alpha)
    y_s = _peer(h_s, wq_t, k1, k2, u_bf, vt_bf, g2, b2, alpha).reshape(nb_, n_new, d)

    u_ext_s = jnp.concatenate([st, u_s.reshape(nb_, n_new, d)], axis=1)
    conv_state_s = u_ext_s[:, u_ext_s.shape[1] - (CONV_WIDTH - 1):][None]

    return (y_p, y_s, kv_p[0], kv_p[1], kv_p[2], conv_state_p,
            kv_s[0], kv_s[1], kv_s[2], conv_state_s)
```

```python
import functools
import math

import numpy as np
import jax
import jax.numpy as jnp
from jax import lax
from jax.experimental import pallas as pl
from jax.experimental.pallas import tpu as pltpu

F32 = jnp.float32
BF16 = jnp.bfloat16

LANES = 128
MXU_WIDTH = 256
D_MODEL = 1024
N_GROUPS = 3
N_SLOTS = 8
HEAD_DIM = 64
GROUP_W = N_SLOTS * HEAD_DIM
GROUP_SLABS = GROUP_W // LANES
ATT_W = N_GROUPS * GROUP_W
DIL_GROUPS = ((128, 1), (512, 4), (2048, 16))
SPAN = 128
BAND_BLOCK = 128
ROPE_THETA = 10000.0
CONV_WIDTH = 3
PEER_HEADS = 8
PEER_N_KEYS = 128
PEER_D_KEY = 256
PEER_TOPK = 16
LN_EPS = 1e-5
NEG_INF = -1e30
TOKEN_CHUNK = 256
PEER_EXPERT_BLOCK = 2048
RESIDUE_UNROLL = 4

COL_TILE = 512
_QK_TILE0, _V_TILE0, _CIN_TILE0, _CB_TILE0, _CC_TILE0, _GA_TILE0 = 0, 6, 9, 11, 13, 15

VMEM_LIMIT = 56 * 1024 * 1024


def _cparams(sem):
    return pltpu.CompilerParams(dimension_semantics=sem, vmem_limit_bytes=VMEM_LIMIT)


def _col_tiles(w_ref):
    return [slice(c0, c0 + COL_TILE) for c0 in range(0, w_ref.shape[1], COL_TILE)]


def _proj_rope_kernel(x_ref, w_ref, cos_ref, sin_ref, o_ref, *, n_q_tiles):
    xb = x_ref[...].astype(BF16)
    cos = cos_ref[...]
    sin = sin_ref[...]
    lane = lax.broadcasted_iota(jnp.int32, cos.shape, 1)
    first_half = (lane & (HEAD_DIM // 2)) == 0
    for j, cs in enumerate(_col_tiles(w_ref)):
        acc = jnp.dot(xb, w_ref[:, cs], preferred_element_type=F32)
        scale = HEAD_DIM ** -0.5 if j < n_q_tiles else 1.0
        for c in range(COL_TILE // LANES):
            a = acc[:, c * LANES:(c + 1) * LANES]
            partner = jnp.where(first_half, pltpu.roll(a, LANES - HEAD_DIM // 2, 1), pltpu.roll(a, HEAD_DIM // 2, 1))
            rot = a * cos + partner * sin
            o_ref[j * (COL_TILE // LANES) + c] = rot * scale if scale != 1.0 else rot


def _proj_slab_kernel(x_ref, w_ref, o_ref):
    xb = x_ref[...].astype(BF16)
    for j, cs in enumerate(_col_tiles(w_ref)):
        acc = jnp.dot(xb, w_ref[:, cs], preferred_element_type=F32)
        for c in range(COL_TILE // LANES):
            o_ref[j * (COL_TILE // LANES) + c] = acc[:, c * LANES:(c + 1) * LANES]


def _proj_kernel(x_ref, w_ref, o_ref):
    xb = x_ref[...].astype(BF16)
    for cs in _col_tiles(w_ref):
        o_ref[:, cs] = jnp.dot(xb, w_ref[:, cs], preferred_element_type=F32)


def _proj_product_kernel(x_ref, w1_ref, w2_ref, o_ref):
    xb = x_ref[...].astype(BF16)
    for cs in _col_tiles(w1_ref):
        o_ref[:, cs] = (jnp.dot(xb, w1_ref[:, cs], preferred_element_type=F32)
                        * jnp.dot(xb, w2_ref[:, cs], preferred_element_type=F32))


def _input_projections(x, w_bf, cos_t, sin_t):
    t = x.shape[0]
    tm = min(1024, t)
    nt = t // tm
    n_pos_tiles = cos_t.shape[0] // tm
    x_spec = pl.BlockSpec((tm, D_MODEL), lambda i: (i, 0))
    pos_spec = pl.BlockSpec((tm, LANES), lambda i: (i % n_pos_tiles, 0))
    sem = ("parallel",)

    def cols(tile0, n_tiles):
        return w_bf[:, tile0 * COL_TILE:(tile0 + n_tiles) * COL_TILE]

    def w_spec(w):
        return pl.BlockSpec(w.shape, lambda i: (0, 0))

    def row_out(n_cols):
        return pl.BlockSpec((tm, n_cols), lambda i: (i, 0)), jax.ShapeDtypeStruct((t, n_cols), F32)

    def slab_out(n_cols):
        return (pl.BlockSpec((n_cols // LANES, tm, LANES), lambda i: (0, i, 0)),
                jax.ShapeDtypeStruct((n_cols // LANES, t, LANES), F32))

    w_qk = cols(_QK_TILE0, 2 * ATT_W // COL_TILE)
    spec, shape = slab_out(2 * ATT_W)
    qk = pl.pallas_call(
        functools.partial(_proj_rope_kernel, n_q_tiles=ATT_W // COL_TILE), grid=(nt,),
        in_specs=[x_spec, w_spec(w_qk), pos_spec, pos_spec], out_specs=spec, out_shape=shape,
        compiler_params=_cparams(sem), name="proj_qk_rope",
    )(x, w_qk, cos_t, sin_t)

    w_v = cols(_V_TILE0, ATT_W // COL_TILE)
    spec, shape = slab_out(ATT_W)
    v = pl.pallas_call(
        _proj_slab_kernel, grid=(nt,), in_specs=[x_spec, w_spec(w_v)], out_specs=spec, out_shape=shape,
        compiler_params=_cparams(sem), name="proj_v",
    )(x, w_v)

    w_cin, w_cc = cols(_CIN_TILE0, D_MODEL // COL_TILE), cols(_CC_TILE0, D_MODEL // COL_TILE)
    spec, shape = row_out(D_MODEL)
    u = pl.pallas_call(
        _proj_product_kernel, grid=(nt,), in_specs=[x_spec, w_spec(w_cin), w_spec(w_cc)],
        out_specs=spec, out_shape=shape, compiler_params=_cparams(sem), name="proj_conv_u",
    )(x, w_cin, w_cc)

    w_rest = jnp.concatenate([cols(_CB_TILE0, D_MODEL // COL_TILE), cols(_GA_TILE0, 2 * D_MODEL // COL_TILE)], axis=1)
    spec, shape = row_out(3 * D_MODEL)
    rest = pl.pallas_call(
        _proj_kernel, grid=(nt,), in_specs=[x_spec, w_spec(w_rest)], out_specs=spec, out_shape=shape,
        compiler_params=_cparams(sem), name="proj_cb_gates",
    )(x, w_rest)
    return qk, v, u, rest


def _band_attn_kernel(q_ref, kp_ref, kc_ref, vp_ref, vc_ref, o_ref, l_ref, *, dil, nbb):
    b = pl.program_id(2)
    qi = lax.broadcasted_iota(jnp.int32, (BAND_BLOCK, 2 * BAND_BLOCK), 0)
    ki = lax.broadcasted_iota(jnp.int32, (BAND_BLOCK, 2 * BAND_BLOCK), 1)
    dist = BAND_BLOCK + qi - ki
    in_band = (dist >= 0) & (dist <= SPAN)
    first_mask = in_band & (ki >= jnp.where(b > 0, 0, BAND_BLOCK))

    def rows(ref, start):
        if dil == 1:
            return ref[pl.ds(start, BAND_BLOCK), :]
        return ref[pl.ds(start, BAND_BLOCK, stride=dil), :]

    def residue(r):
        for bi in range(nbb):
            base = bi * BAND_BLOCK * dil + r
            q = rows(q_ref, base).astype(BF16)
            if bi == 0:
                kp, vp = rows(kp_ref, r), rows(vp_ref, r)
            else:
                kp, vp = rows(kc_ref, base - BAND_BLOCK * dil), rows(vc_ref, base - BAND_BLOCK * dil)
            k = jnp.concatenate([kp, rows(kc_ref, base)], axis=0).astype(BF16)
            v = jnp.concatenate([vp, rows(vc_ref, base)], axis=0).astype(BF16)
            mask = first_mask if bi == 0 else in_band
            outs, lses = [], []
            for hl in range(LANES // HEAD_DIM):
                sl = slice(hl * HEAD_DIM, (hl + 1) * HEAD_DIM)
                s = lax.dot_general(q[:, sl], k[:, sl], (((1,), (1,)), ((), ())), preferred_element_type=F32)
                s = jnp.where(mask, s, NEG_INF)
                mx = jnp.max(s, axis=-1, keepdims=True)
                p = jnp.exp(s - mx)
                den = jnp.sum(p, axis=-1, keepdims=True)
                outs.append(jnp.dot(p.astype(BF16), v[:, sl], preferred_element_type=F32) / den)
                lses.append(jnp.broadcast_to(mx + jnp.log(den), (BAND_BLOCK, HEAD_DIM)))
            o = jnp.concatenate(outs, axis=1)
            l = jnp.concatenate(lses, axis=1)
            if dil == 1:
                o_ref[pl.ds(base, BAND_BLOCK), :] = o
                l_ref[pl.ds(base, BAND_BLOCK), :] = l
            else:
                o_ref[pl.ds(base, BAND_BLOCK, stride=dil), :] = o
                l_ref[pl.ds(base, BAND_BLOCK, stride=dil), :] = l

    if dil <= RESIDUE_UNROLL:
        for r in range(dil):
            residue(r)
    else:
        def residue_group(rg, carry):
            for ru in range(RESIDUE_UNROLL):
                residue(rg * RESIDUE_UNROLL + ru)
            return carry

        lax.fori_loop(0, dil // RESIDUE_UNROLL, residue_group, 0)


def _prompt_attention(qk, v, n, s, gi, dil):
    band_rows = BAND_BLOCK * dil
    nbb = max(512 // band_rows, 1)
    rows = band_rows * nbb
    qk4 = qk.reshape(qk.shape[0], n, s, LANES)
    v4 = v.reshape(v.shape[0], n, s, LANES)
    cur = (None, None, rows, LANES)
    prv = (None, None, band_rows, LANES)
    q0 = gi * GROUP_SLABS
    k0 = (N_GROUPS + gi) * GROUP_SLABS
    v0 = gi * GROUP_SLABS

    def prev(b):
        return jnp.maximum(b * nbb - 1, 0)

    in_specs = [
        pl.BlockSpec(cur, lambda a, c, b: (q0 + c, a, b, 0)),
        pl.BlockSpec(prv, lambda a, c, b: (k0 + c, a, prev(b), 0)),
        pl.BlockSpec(cur, lambda a, c, b: (k0 + c, a, b, 0)),
        pl.BlockSpec(prv, lambda a, c, b: (v0 + c, a, prev(b), 0)),
        pl.BlockSpec(cur, lambda a, c, b: (v0 + c, a, b, 0)),
    ]
    o_spec = pl.BlockSpec(cur, lambda a, c, b: (c, a, b, 0))
    out, lse = pl.pallas_call(
        functools.partial(_band_attn_kernel, dil=dil, nbb=nbb), grid=(n, GROUP_SLABS, s // rows),
        in_specs=in_specs, out_specs=[o_spec, o_spec],
        out_shape=[jax.ShapeDtypeStruct((GROUP_SLABS, n, s, LANES), F32)] * 2,
        compiler_params=_cparams(("parallel", "parallel", "arbitrary")), name=f"band_attn_d{dil}",
    )(qk4, qk4, qk4, v4, v4)
    return out.reshape(GROUP_SLABS, n * s, LANES), lse.reshape(GROUP_SLABS, n * s, LANES)


def _cached_attn_kernel(q_ref, c_ref, n_ref, bias_ref, co_ref, o_ref, l_ref, *, win, bb, n_new):
    bias = bias_ref[...]
    lane = lax.broadcasted_iota(jnp.int32, (HEAD_DIM, LANES), 1)
    is_new = lane >= LANES - n_new

    def body(b, carry):
        for h in range(N_SLOTS):
            hs, hl = h // 2, h % 2
            sl = slice(hl * HEAD_DIM, (hl + 1) * HEAD_DIM)
            q = q_ref[hs, b][:, sl]
            kt = jnp.concatenate([c_ref[b, 0, h], n_ref[b, 0, h]], axis=1)
            vt = jnp.concatenate([c_ref[b, 1, h], n_ref[b, 1, h]], axis=1)
            s = jnp.dot(q, kt, preferred_element_type=F32) + bias
            mx = jnp.max(s, axis=-1, keepdims=True)
            p = jnp.exp(s - mx)
            den = jnp.sum(p, axis=-1, keepdims=True)
            o = lax.dot_general(p, vt, (((1,), (1,)), ((), ())), preferred_element_type=F32) / den
            o_ref[hs, b, :, sl] = o
            l_ref[hs, b, :, sl] = jnp.broadcast_to(mx + jnp.log(den), (n_new, HEAD_DIM))
            for kv in range(2):
                rolled = pltpu.roll(c_ref[b, kv, h], win - n_new, 1)
                if win > LANES:
                    co_ref[b, kv, h, :, :win - LANES] = rolled[:, :win - LANES]
                co_ref[b, kv, h, :, win - LANES:] = jnp.where(is_new, n_ref[b, kv, h], rolled[:, win - LANES:])
        return carry

    lax.fori_loop(0, bb, body, 0)


def _cache_bias(win, dil, n_new):
    t = np.arange(n_new)[:, None]
    j = np.arange(win)[None, :]
    dist = win + t - j
    ok_c = (dist % dil == 0) & (dist <= dil * SPAN)
    l = np.arange(LANES)[None, :]
    tn = l - (LANES - n_new)
    ok_n = (tn >= 0) & (tn <= t) & ((t - tn) % dil == 0)
    ok = np.concatenate([ok_c, ok_n], axis=1)
    return jnp.asarray(np.where(ok, 0.0, NEG_INF), dtype=F32)


def _sample_attention(qk, new_t, cache, nb_, n_new, gi, win, dil):
    bb = max(1, min(8, 2048 // win))
    while nb_ % bb:
        bb //= 2
    qk4 = qk.reshape(qk.shape[0], nb_, n_new, LANES)
    cache_t = jnp.transpose(cache, (0, 2, 3, 4, 1))
    bias = _cache_bias(win, dil, n_new)
    c_blk = (bb, 2, N_SLOTS, HEAD_DIM, win)
    n_blk = (bb, 2, N_SLOTS, HEAD_DIM, LANES)
    q_blk = (GROUP_SLABS, bb, n_new, LANES)
    in_specs = [
        pl.BlockSpec(q_blk, lambda i: (gi, i, 0, 0)),
        pl.BlockSpec(c_blk, lambda i: (i, 0, 0, 0, 0)),
        pl.BlockSpec(n_blk, lambda i: (i, 0, 0, 0, 0)),
        pl.BlockSpec(bias.shape, lambda i: (0, 0)),
    ]
    o_spec = pl.BlockSpec(q_blk, lambda i: (0, i, 0, 0))
    cache_o, out, lse = pl.pallas_call(
        functools.partial(_cached_attn_kernel, win=win, bb=bb, n_new=n_new),
        grid=(nb_ // bb,),
        in_specs=in_specs,
        out_specs=[pl.BlockSpec(c_blk, lambda i: (i, 0, 0, 0, 0)), o_spec, o_spec],
        out_shape=[jax.ShapeDtypeStruct(cache_t.shape, F32),
                   jax.ShapeDtypeStruct((GROUP_SLABS, nb_, n_new, LANES), F32),
                   jax.ShapeDtypeStruct((GROUP_SLABS, nb_, n_new, LANES), F32)],
        compiler_params=_cparams(("parallel",)), name=f"cached_attn_d{dil}",
    )(qk4, cache_t, new_t, bias)
    new_cache = jnp.transpose(cache_o, (0, 4, 1, 2, 3))
    return (out.reshape(GROUP_SLABS, nb_ * n_new, LANES), lse.reshape(GROUP_SLABS, nb_ * n_new, LANES), new_cache)


def _layer_norm(x, g, b):
    mu = jnp.mean(x, axis=-1, keepdims=True)
    xc = x - mu
    var = jnp.mean(xc * xc, axis=-1, keepdims=True)
    return xc * lax.rsqrt(var + LN_EPS) * g + b


def _mix_kernel(x_ref, o0_ref, o1_ref, o2_ref, l0_ref, l1_ref, l2_ref, u_ref, h1_ref, h2_ref,
                cb_ref, ga_ref, gc_ref, cw_ref, wpa_ref, wpc_ref, wo_ref, g_ref, b_ref, out_ref,
                *, seg_len, alpha):
    tm = x_ref.shape[0]
    parts = []
    for c in range(GROUP_SLABS):
        l0, l1, l2 = l0_ref[c], l1_ref[c], l2_ref[c]
        lm = jnp.maximum(jnp.maximum(l0, l1), l2)
        w0, w1, w2 = jnp.exp(l0 - lm), jnp.exp(l1 - lm), jnp.exp(l2 - lm)
        parts.append((w0 * o0_ref[c] + w1 * o1_ref[c] + w2 * o2_ref[c]) / (w0 + w1 + w2))
    y_att = jnp.concatenate(parts, axis=1)
    u = u_ref[...]
    row = lax.broadcasted_iota(jnp.int32, u.shape, 0)
    if seg_len >= tm:
        keep = jnp.where((pl.program_id(0) * tm) % seg_len == 0, 0.0, 1.0).astype(F32)
        hrow1 = h1_ref[7:8, :] * keep
        hrow2 = h1_ref[6:7, :] * keep
        prev1 = jnp.where(row == 0, hrow1, pltpu.roll(u, 1, 0))
        prev2 = jnp.where(row == 0, hrow2, jnp.where(row == 1, hrow1, pltpu.roll(u, 2, 0)))
    else:
        rs = row & (seg_len - 1)
        prev1 = jnp.where(rs == 0, h1_ref[...], pltpu.roll(u, 1, 0))
        prev2 = jnp.where(rs < 2, h2_ref[...], pltpu.roll(u, 2, 0))
    cw = cw_ref[...]
    z_conv = prev2 * cw[0:1, :] + prev1 * cw[1:2, :] + u * cw[2:3, :]
    cv = cb_ref[...] * z_conv
    br_a = jnp.dot(y_att.astype(BF16), wpa_ref[...], preferred_element_type=F32)
    br_c = jnp.dot(cv.astype(BF16), wpc_ref[...], preferred_element_type=F32)
    merged = jax.nn.sigmoid(ga_ref[...]) * br_a + jax.nn.sigmoid(gc_ref[...]) * br_c
    mix = jnp.dot(merged.astype(BF16), wo_ref[...], preferred_element_type=F32)
    out_ref[...] = _layer_norm(alpha * x_ref[...] + mix, g_ref[...], b_ref[...])


def _mixer_finish(x, outs, lses, u, rest, hist1, hist2, seg_len, conv_w, wpa, wpc, wo, ln_g, ln_b, alpha):
    t = x.shape[0]
    tm = min(512, t)
    nt = t // tm
    row_spec = pl.BlockSpec((tm, D_MODEL), lambda i: (i, 0))
    att_spec = pl.BlockSpec((GROUP_SLABS, tm, LANES), lambda i: (0, i, 0))
    if seg_len >= tm:
        hb = tm // 8
        h1_spec = pl.BlockSpec((8, D_MODEL), lambda i: (jnp.maximum(i * hb - 1, 0), 0))
        h2_spec = h1_spec
    else:
        h1_spec = h2_spec = row_spec

    def full(shape):
        return pl.BlockSpec(shape, lambda i: (0,) * len(shape))

    in_specs = ([row_spec] + [att_spec] * 6 + [row_spec, h1_spec, h2_spec]
                + [pl.BlockSpec((tm, D_MODEL), lambda i, c=c: (i, c)) for c in range(3)]
                + [full((CONV_WIDTH, D_MODEL)), full((GROUP_W, D_MODEL)), full((D_MODEL, D_MODEL)),
                   full((D_MODEL, D_MODEL)), full((1, D_MODEL)), full((1, D_MODEL))])
    return pl.pallas_call(
        functools.partial(_mix_kernel, seg_len=seg_len, alpha=alpha),
        grid=(nt,), in_specs=in_specs, out_specs=row_spec,
        out_shape=jax.ShapeDtypeStruct((t, D_MODEL), F32),
        compiler_params=_cparams(("parallel",)), name="mixer_finish",
    )(x, *outs, *lses, u, hist1, hist2, rest, rest, rest, conv_w, wpa, wpc, wo, ln_g, ln_b)


def _ranked_top16(s, exact_ties):
    n_keys = s.shape[0]
    rowf = lax.broadcasted_iota(jnp.int32, s.shape, 0).astype(F32)
    rank = jnp.full(s.shape, float(PEER_N_KEYS - 1), F32)
    tops = []
    for k in range(PEER_TOPK):
        m = jnp.max(s, axis=0, keepdims=True)
        if exact_ties:
            pos = jnp.min(jnp.where(s == m, rowf, float(n_keys)), axis=0, keepdims=True)
            hit = rowf == pos
        else:
            hit = s == m
        s = jnp.where(hit, -jnp.inf, s)
        rank = jnp.where(hit, float(k), rank)
        tops.append(m)
    return rank, tops


def _not_16(flags):
    total = jnp.sum(flags, axis=0, keepdims=True)
    return jnp.where(total == float(PEER_TOPK), 0.0, 1.0)


def _rows_to_tile(rows):
    tmw = rows[0].shape[1]
    ri = lax.broadcasted_iota(jnp.int32, (8, tmw), 0)
    out = jnp.zeros((8, tmw), F32)
    for a, rw in enumerate(rows):
        out = jnp.where(ri == a, rw, out)
    return out


def _select_counts(t1, t2, exact_ties):
    tmw = t1[0].shape[1]
    ri = lax.broadcasted_iota(jnp.int32, (8, tmw), 0)
    rif = ri.astype(F32)
    ninf = -jnp.inf
    v1lo, v1hi = _rows_to_tile(t1[:8]), _rows_to_tile(t1[8:])
    v2lo, v2hi = _rows_to_tile(t2[:8]), _rows_to_tile(t2[8:])
    all_rows = ri >= 0
    tiles = [
        (t1[0] + v2lo, rif, all_rows),
        (t1[0] + v2hi, 8.0 + rif, all_rows),
        (t1[1] + v2lo, 16.0 + rif, all_rows),
        (v1hi + t2[0], (8.0 + rif) * 16.0, all_rows),
        (v1lo + t2[0], rif * 16.0, ri >= 2),
        (v1lo + t2[1], rif * 16.0 + 1.0, ri >= 2),
        (t1[2] + v2lo, 32.0 + rif, (ri >= 2) & (ri <= 4)),
        (t1[3] + v2lo, 48.0 + rif, (ri >= 2) & (ri <= 3)),
        (t1[4] + v2lo, 64.0 + rif, ri == 2),
    ]
    vals = [jnp.where(ok, sm, ninf) for sm, _, ok in tiles]
    poss = [jnp.where(ok, ps, -1.0) for _, ps, ok in tiles]
    hits = [jnp.zeros((8, tmw), F32) for _ in vals]
    top = t1[0] + t2[0]
    z = jnp.zeros((1, tmw), F32)
    big = 1e9
    for _ in range(PEER_TOPK):
        m8 = vals[0]
        for vv in vals[1:]:
            m8 = jnp.maximum(m8, vv)
        m = jnp.max(m8, axis=0, keepdims=True)
        if exact_ties:
            p8 = jnp.where(vals[0] == m, poss[0], big)
            for vv, pp in zip(vals[1:], poss[1:]):
                p8 = jnp.minimum(p8, jnp.where(vv == m, pp, big))
            pm = jnp.min(p8, axis=0, keepdims=True)
        for c in range(len(vals)):
            hit = (poss[c] == pm) if exact_ties else (vals[c] == m)
            vals[c] = jnp.where(hit, ninf, vals[c])
            hits[c] = jnp.where(hit, 1.0, hits[c])
        z = z + jnp.exp(m - top)

    def colsum(x):
        return jnp.sum(x, axis=0, keepdims=True)

    n_picked = hits[0]
    for hh in hits[1:]:
        n_picked = n_picked + hh
    not16 = _not_16(n_picked)
    lo = hits[4] + hits[5]
    counts = [colsum(hits[0]) + colsum(hits[1]), colsum(hits[2]),
              lo[2:3] + colsum(hits[6]), lo[3:4] + colsum(hits[7]), lo[4:5] + colsum(hits[8]),
              lo[5:6], lo[6:7], lo[7:8]]
    counts += [hits[3][a:a + 1] for a in range(8)]
    return counts, z, not16


def _packed_rows(row, n_rows):
    tile = jnp.broadcast_to(row, (16, row.shape[1])).astype(BF16)
    return jnp.concatenate([tile] * (n_rows // 16), axis=0)


def _peer_kernel(hn_ref, hp_ref, wq_ref, k1_ref, k2_ref, u_ref, vt_ref, g_ref, b_ref, out_ref,
                 ht_ref, c1_ref, e1_ref, r2_ref, e2_ref, a_ref, wt_ref, ft_ref, *, alpha, eb, n_chunks,
                 parts_per_step):
    i = pl.program_id(0)
    e = pl.program_id(1)
    n_e = pl.num_programs(1)
    half = PEER_D_KEY // 2
    rows_per_step = eb // PEER_N_KEYS
    cw = ht_ref.shape[-1]
    sel_slot = i % 2
    use_slot = (i + 1) % 2

    @pl.when((i == 0) & (e == 0))
    def _init():
        ht_ref[1] = jnp.zeros(ht_ref.shape[1:], BF16)
        c1_ref[1] = jnp.zeros(c1_ref.shape[1:], F32)
        e1_ref[1] = jnp.zeros(e1_ref.shape[1:], F32)
        r2_ref[1] = jnp.zeros(r2_ref.shape[1:], BF16)
        e2_ref[1] = jnp.zeros(e2_ref.shape[1:], BF16)

    @pl.when(e == 0)
    def _start_tile():
        ht = hn_ref[...].T.astype(BF16)
        for c in range(n_chunks):
            ht_ref[sel_slot, c] = ht[:, c * cw:(c + 1) * cw]
        ft_ref[...] = jnp.zeros(ft_ref.shape, F32)

    n_parts = PEER_HEADS * n_chunks

    def select_part(k, exact_ties):
        part = (e * parts_per_step + k) % n_parts
        hh = part // n_chunks
        sc = part % n_chunks
        off = pl.multiple_of(hh * PEER_D_KEY, PEER_D_KEY)
        qy = jnp.dot(wq_ref[pl.ds(off, PEER_D_KEY), :], ht_ref[sel_slot, sc], preferred_element_type=F32)
        s1 = jnp.dot(k1_ref[...], qy[:half].astype(BF16), preferred_element_type=F32)
        s2 = jnp.dot(k2_ref[...], qy[half:].astype(BF16), preferred_element_type=F32)
        suspect = jnp.zeros((1, LANES), F32)
        for l0 in range(0, cw, LANES):
            ls = slice(l0, l0 + LANES)
            s1c, s2c = s1[:, ls], s2[:, ls]
            e1_ref[sel_slot, sc, hh, :, ls] = jnp.exp(s1c - jnp.max(s1c, axis=0, keepdims=True))
            e2_ref[sel_slot, sc, hh, :, ls] = jnp.exp(s2c - jnp.max(s2c, axis=0, keepdims=True)).astype(BF16)
            r2, t2 = _ranked_top16(s2c, exact_ties)
            r2_ref[sel_slot, sc, hh, :, ls] = r2.astype(BF16)
            r1, t1 = _ranked_top16(s1c, exact_ties)
            counts, z, not16 = _select_counts(t1, t2, exact_ties)
            c1 = jnp.zeros(s1c.shape, F32)
            for a in range(PEER_TOPK):
                c1 = jnp.where(r1 == float(a), counts[a], c1)
            c1_ref[sel_slot, sc, hh, :, ls] = c1
            e1_ref[sel_slot, sc, hh, :, ls] = e1_ref[sel_slot, sc, hh, :, ls] / z
            if not exact_ties:
                ranked = float(PEER_N_KEYS - 1)
                suspect = jnp.maximum(suspect, not16)
                suspect = jnp.maximum(suspect, _not_16(jnp.where(r1 < ranked, 1.0, 0.0)))
                suspect = jnp.maximum(suspect, _not_16(jnp.where(r2 < ranked, 1.0, 0.0)))
        return suspect

    suspect = select_part(0, exact_ties=False)
    for k in range(1, parts_per_step):
        suspect = jnp.maximum(suspect, select_part(k, exact_ties=False))
    any_suspect = jnp.max(suspect)

    for c in range(n_chunks):
        a_ref[c] = jnp.dot(u_ref[...], ht_ref[use_slot, c], preferred_element_type=F32)
    for c in range(n_chunks):
        for il in range(rows_per_step):
            row = e * rows_per_step + il
            rs = slice(il * PEER_N_KEYS, (il + 1) * PEER_N_KEYS)
            pre = a_ref[c, rs, :]
            act = 0.5 * pre * (1.0 + lax.erf(pre * math.sqrt(0.5)))
            gate = jnp.zeros(pre.shape, BF16)
            for h2 in range(PEER_HEADS):
                cnt = _packed_rows(c1_ref[use_slot, c, h2, pl.ds(row, 1), :], PEER_N_KEYS)
                wgt = _packed_rows(e1_ref[use_slot, c, h2, pl.ds(row, 1), :], PEER_N_KEYS)
                gate = gate + jnp.where(r2_ref[use_slot, c, h2] < cnt, e2_ref[use_slot, c, h2],
                                        jnp.zeros((), BF16)) * wgt
            wt_ref[c, rs, :] = gate * act.astype(BF16)
        ft_ref[c] += jnp.dot(vt_ref[...], wt_ref[c], preferred_element_type=F32)

    @pl.when(any_suspect > 0.0)
    def _redo_with_ties():
        for k in range(parts_per_step):
            select_part(k, exact_ties=True)

    @pl.when((e == n_e - 1) & (i > 0))
    def _finish():
        f = jnp.concatenate([ft_ref[c].T for c in range(n_chunks)], axis=0)
        out_ref[...] = _layer_norm(alpha * hp_ref[...] + f, g_ref[...], b_ref[...])


def _peer(h, wq_t, k1, k2, u_bf, vt_bf, ln_g, ln_b, alpha):
    t = h.shape[0]
    tm = min(512, t)
    eb = PEER_EXPERT_BLOCK
    n_exp = u_bf.shape[0]
    nt = t // tm
    n_e = n_exp // eb
    cw = min(TOKEN_CHUNK, tm)
    n_chunks = tm // cw
    n_parts = PEER_HEADS * n_chunks
    parts_per_step = -(-n_parts // n_e)
    assert (n_e * parts_per_step) % n_parts == 0

    def full(shape):
        return pl.BlockSpec(shape, lambda i, e: (0,) * len(shape))

    next_spec = pl.BlockSpec((tm, D_MODEL), lambda i, e: (jnp.minimum(i, nt - 1), 0))
    prev_spec = pl.BlockSpec((tm, D_MODEL), lambda i, e: (jnp.maximum(i - 1, 0), 0))
    meta32 = pltpu.VMEM((2, n_chunks, PEER_HEADS, PEER_N_KEYS, cw), F32)
    meta16 = pltpu.VMEM((2, n_chunks, PEER_HEADS, PEER_N_KEYS, cw), BF16)
    return pl.pallas_call(
        functools.partial(_peer_kernel, alpha=alpha, eb=eb, n_chunks=n_chunks, parts_per_step=parts_per_step),
        grid=(nt + 1, n_e),
        in_specs=[next_spec, prev_spec, full(wq_t.shape), full(k1.shape), full(k2.shape),
                  pl.BlockSpec((eb, D_MODEL), lambda i, e: (e, 0)),
                  pl.BlockSpec((D_MODEL, eb), lambda i, e: (0, e)),
                  full((1, D_MODEL)), full((1, D_MODEL))],
        out_specs=prev_spec,
        out_shape=jax.ShapeDtypeStruct((t, D_MODEL), F32),
        scratch_shapes=[pltpu.VMEM((2, n_chunks, D_MODEL, cw), BF16), meta32, meta32, meta16, meta16,
                        pltpu.VMEM((n_chunks, eb, cw), F32), pltpu.VMEM((n_chunks, eb, cw), BF16),
                        pltpu.VMEM((n_chunks, D_MODEL, cw), F32)],
        compiler_params=_cparams(("arbitrary", "arbitrary")), name="peer",
    )(h, h, wq_t, k1, k2, u_bf, vt_bf, ln_g, ln_b)


def _rope_tables(pos):
    half = HEAD_DIM // 2
    inv_freq = ROPE_THETA ** (-jnp.arange(half, dtype=F32) / half)
    ang = pos.astype(F32)[:, None] * inv_freq[None, :]
    cos, sin = jnp.cos(ang), jnp.sin(ang)
    cos_t = jnp.concatenate([cos, cos, cos, cos], axis=-1)
    sin_t = jnp.concatenate([-sin, sin, -sin, sin], axis=-1)
    return cos_t, sin_t


def _slab_rows_to_heads(slabs, lead):
    nl = len(lead)
    x = jnp.moveaxis(slabs, 0, nl + 1)
    return x.reshape(*lead, x.shape[nl], N_SLOTS, HEAD_DIM)


def kernel(x_prompt, x_sample, cache_kv_w128, cache_kv_w512, cache_kv_w2048, state_conv, w_in, conv_w, w_branch_attn, w_branch_conv, w_out, ln1_g, ln1_b, peer_w_q, peer_keys_1, peer_keys_2, peer_u, peer_v, ln2_g, ln2_b):
    depth = w_in.shape[0]
    assert depth == 1, "single-layer trunk"
    n, s, d = x_prompt.shape
    nb_, n_new, _ = x_sample.shape
    past_len = 8192
    alpha = (2.0 * depth) ** 0.25
    caches = (cache_kv_w128, cache_kv_w512, cache_kv_w2048)
    for (win, dil), c in zip(DIL_GROUPS, caches):
        assert c.shape[2] == win and s % (dil * BAND_BLOCK) == 0 and win // dil == SPAN and win <= s

    lyr = 0
    w_bf = w_in[lyr].astype(BF16)
    wpa = w_branch_attn[lyr].astype(BF16)
    wpc = w_branch_conv[lyr].astype(BF16)
    wo = w_out[lyr].astype(BF16)
    wq_t = peer_w_q[lyr].T.astype(BF16)
    k1 = peer_keys_1[lyr].astype(BF16)
    k2 = peer_keys_2[lyr].astype(BF16)
    u_bf = peer_u[lyr].astype(BF16)
    vt_bf = peer_v[lyr].T.astype(BF16)
    g1, b1 = ln1_g[lyr][None, :], ln1_b[lyr][None, :]
    g2, b2 = ln2_g[lyr][None, :], ln2_b[lyr][None, :]
    cw = conv_w[lyr]
    k_slab0 = N_GROUPS * GROUP_SLABS

    xp = x_prompt.reshape(n * s, d)
    cos_p, sin_p = _rope_tables(jnp.arange(s))
    qk_p, v_p, u_p, rest_p = _input_projections(xp, w_bf, cos_p, sin_p)
    outs, lses = [], []
    for gi, (win, dil) in enumerate(DIL_GROUPS):
        o, l = _prompt_attention(qk_p, v_p, n, s, gi, dil)
        outs.append(o)
        lses.append(l)
    h_p = _mixer_finish(xp, outs, lses, u_p, rest_p, u_p, u_p, s, cw, wpa, wpc, wo, g1, b1, alpha)
    y_p = _peer(h_p, wq_t, k1, k2, u_bf, vt_bf, g2, b2, alpha).reshape(n, s, d)

    qk_p4 = qk_p.reshape(qk_p.shape[0], n, s, LANES)
    v_p4 = v_p.reshape(v_p.shape[0], n, s, LANES)
    kv_p = []
    for gi, (win, dil) in enumerate(DIL_GROUPS):
        ks = k_slab0 + gi * GROUP_SLABS
        vs = gi * GROUP_SLABS
        k_tail = _slab_rows_to_heads(qk_p4[ks:ks + GROUP_SLABS, :, s - win:], (n,))
        v_tail = _slab_rows_to_heads(v_p4[vs:vs + GROUP_SLABS, :, s - win:], (n,))
        kv_p.append(jnp.stack([k_tail, v_tail], axis=2)[None])
    conv_state_p = u_p.reshape(n, s, d)[:, s - (CONV_WIDTH - 1):][None]

    xs = x_sample.reshape(nb_ * n_new, d)
    cos_s, sin_s = _rope_tables(jnp.tile(past_len + jnp.arange(n_new), nb_))
    qk_s, v_s, u_s, rest_s = _input_projections(xs, w_bf, cos_s, sin_s)
    qk_s4 = qk_s.reshape(qk_s.shape[0], nb_, n_new, LANES)
    v_s4 = v_s.reshape(v_s.shape[0], nb_, n_new, LANES)
    outs, lses, kv_s = [], [], []
    for gi, (win, dil) in enumerate(DIL_GROUPS):
        ks = k_slab0 + gi * GROUP_SLABS
        vs = gi * GROUP_SLABS
        k_new = _slab_rows_to_heads(qk_s4[ks:ks + GROUP_SLABS], (nb_,))
        v_new = _slab_rows_to_heads(v_s4[vs:vs + GROUP_SLABS], (nb_,))
        new_t = jnp.transpose(jnp.stack([k_new, v_new], axis=1), (0, 1, 3, 4, 2))
        new_t = jnp.pad(new_t, ((0, 0),) * 4 + ((LANES - n_new, 0),))
        o, l, cache_new = _sample_attention(qk_s, new_t, caches[gi][lyr], nb_, n_new, gi, win, dil)
        outs.append(o)
        lses.append(l)
        kv_s.append(cache_new[None])
    st = state_conv[lyr]
    zeros6 = jnp.zeros((nb_, n_new - 1, d), F32)
    hist1 = jnp.concatenate([st[:, 1:2], zeros6], axis=1).reshape(nb_ * n_new, d)
    hist2 = jnp.concatenate([st, zeros6[:, 1:]], axis=1).reshape(nb_ * n_new, d)
    h_s = _mixer_finish(xs, outs, lses, u_s, rest_s, hist1, hist2, n_new, cw, wpa, wpc, wo, g1, b1, alpha)
    y_s = _peer(h_s, wq_t, k1, k2, u_bf, vt_bf, g2, b2, alpha).reshape(nb_, n_new, d)

    u_ext_s = jnp.concatenate([st, u_s.reshape(nb_, n_new, d)], axis=1)
    conv_state_s = u_ext_s[:, u_ext_s.shape[1] - (CONV_WIDTH - 1):][None]

    return (y_p, y_s, kv_p[0], kv_p[1], kv_p[2], conv_state_p,
            kv_s[0], kv_s[1], kv_s[2], conv_state_s)
```

```python
import functools
import math

import numpy as np
import jax
import jax.numpy as jnp
from jax import lax
from jax.experimental import pallas as pl
from jax.experimental.pallas import tpu as pltpu

F32 = jnp.float32
BF16 = jnp.bfloat16

LANES = 128
MXU_WIDTH = 256
D_MODEL = 1024
N_GROUPS = 3
N_SLOTS = 8
HEAD_DIM = 64
GROUP_W = N_SLOTS * HEAD_DIM
GROUP_SLABS = GROUP_W // LANES
ATT_W = N_GROUPS * GROUP_W
DIL_GROUPS = ((128, 1), (512, 4), (2048, 16))
SPAN = 128
BAND_BLOCK = 128
ROPE_THETA = 10000.0
CONV_WIDTH = 3
PEER_HEADS = 8
PEER_N_KEYS = 128
PEER_D_KEY = 256
PEER_TOPK = 16
LN_EPS = 1e-5
NEG_INF = -1e30
TOKEN_CHUNK = 256
PEER_EXPERT_BLOCK = 2048
RESIDUE_UNROLL = 4

COL_TILE = 512
_QK_TILE0, _V_TILE0, _CIN_TILE0, _CB_TILE0, _CC_TILE0, _GA_TILE0 = 0, 6, 9, 11, 13, 15

VMEM_LIMIT = 56 * 1024 * 1024


def _cparams(sem):
    return pltpu.CompilerParams(dimension_semantics=sem, vmem_limit_bytes=VMEM_LIMIT)


def _col_tiles(w_ref):
    return [slice(c0, c0 + COL_TILE) for c0 in range(0, w_ref.shape[1], COL_TILE)]


def _proj_rope_kernel(x_ref, w_ref, cos_ref, sin_ref, o_ref, *, n_q_tiles):
    xb = x_ref[...].astype(BF16)
    cos = cos_ref[...]
    sin = sin_ref[...]
    lane = lax.broadcasted_iota(jnp.int32, cos.shape, 1)
    first_half = (lane & (HEAD_DIM // 2)) == 0
    for j, cs in enumerate(_col_tiles(w_ref)):
        acc = jnp.dot(xb, w_ref[:, cs], preferred_element_type=F32)
        scale = HEAD_DIM ** -0.5 if j < n_q_tiles else 1.0
        for c in range(COL_TILE // LANES):
            a = acc[:, c * LANES:(c + 1) * LANES]
            partner = jnp.where(first_half, pltpu.roll(a, LANES - HEAD_DIM // 2, 1), pltpu.roll(a, HEAD_DIM // 2, 1))
            rot = a * cos + partner * sin
            o_ref[j * (COL_TILE // LANES) + c] = rot * scale if scale != 1.0 else rot


def _proj_slab_kernel(x_ref, w_ref, o_ref):
    xb = x_ref[...].astype(BF16)
    for j, cs in enumerate(_col_tiles(w_ref)):
        acc = jnp.dot(xb, w_ref[:, cs], preferred_element_type=F32)
        for c in range(COL_TILE // LANES):
            o_ref[j * (COL_TILE // LANES) + c] = acc[:, c * LANES:(c + 1) * LANES]


def _proj_kernel(x_ref, w_ref, o_ref):
    xb = x_ref[...].astype(BF16)
    for cs in _col_tiles(w_ref):
        o_ref[:, cs] = jnp.dot(xb, w_ref[:, cs], preferred_element_type=F32)


def _proj_product_kernel(x_ref, w1_ref, w2_ref, o_ref):
    xb = x_ref[...].astype(BF16)
    for cs in _col_tiles(w1_ref):
        o_ref[:, cs] = (jnp.dot(xb, w1_ref[:, cs], preferred_element_type=F32)
                        * jnp.dot(xb, w2_ref[:, cs], preferred_element_type=F32))


def _input_projections(x, w_bf, cos_t, sin_t):
    t = x.shape[0]
    tm = min(1024, t)
    nt = t // tm
    n_pos_tiles = cos_t.shape[0] // tm
    x_spec = pl.BlockSpec((tm, D_MODEL), lambda i: (i, 0))
    pos_spec = pl.BlockSpec((tm, LANES), lambda i: (i % n_pos_tiles, 0))
    sem = ("parallel",)

    def cols(tile0, n_tiles):
        return w_bf[:, tile0 * COL_TILE:(tile0 + n_tiles) * COL_TILE]

    def w_spec(w):
        return pl.BlockSpec(w.shape, lambda i: (0, 0))

    def row_out(n_cols):
        return pl.BlockSpec((tm, n_cols), lambda i: (i, 0)), jax.ShapeDtypeStruct((t, n_cols), F32)

    def slab_out(n_cols):
        return (pl.BlockSpec((n_cols // LANES, tm, LANES), lambda i: (0, i, 0)),
                jax.ShapeDtypeStruct((n_cols // LANES, t, LANES), F32))

    w_qk = cols(_QK_TILE0, 2 * ATT_W // COL_TILE)
    spec, shape = slab_out(2 * ATT_W)
    qk = pl.pallas_call(
        functools.partial(_proj_rope_kernel, n_q_tiles=ATT_W // COL_TILE), grid=(nt,),
        in_specs=[x_spec, w_spec(w_qk), pos_spec, pos_spec], out_specs=spec, out_shape=shape,
        compiler_params=_cparams(sem), name="proj_qk_rope",
    )(x, w_qk, cos_t, sin_t)

    w_v = cols(_V_TILE0, ATT_W // COL_TILE)
    spec, shape = slab_out(ATT_W)
    v = pl.pallas_call(
        _proj_slab_kernel, grid=(nt,), in_specs=[x_spec, w_spec(w_v)], out_specs=spec, out_shape=shape,
        compiler_params=_cparams(sem), name="proj_v",
    )(x, w_v)

    w_cin, w_cc = cols(_CIN_TILE0, D_MODEL // COL_TILE), cols(_CC_TILE0, D_MODEL // COL_TILE)
    spec, shape = row_out(D_MODEL)
    u = pl.pallas_call(
        _proj_product_kernel, grid=(nt,), in_specs=[x_spec, w_spec(w_cin), w_spec(w_cc)],
        out_specs=spec, out_shape=shape, compiler_params=_cparams(sem), name="proj_conv_u",
    )(x, w_cin, w_cc)

    w_rest = jnp.concatenate([cols(_CB_TILE0, D_MODEL // COL_TILE), cols(_GA_TILE0, 2 * D_MODEL // COL_TILE)], axis=1)
    spec, shape = row_out(3 * D_MODEL)
    rest = pl.pallas_call(
        _proj_kernel, grid=(nt,), in_specs=[x_spec, w_spec(w_rest)], out_specs=spec, out_shape=shape,
        compiler_params=_cparams(sem), name="proj_cb_gates",
    )(x, w_rest)
    return qk, v, u, rest


def _band_attn_kernel(q_ref, kp_ref, kc_ref, vp_ref, vc_ref, o_ref, l_ref, *, dil, nbb):
    b = pl.program_id(2)
    qi = lax.broadcasted_iota(jnp.int32, (BAND_BLOCK, 2 * BAND_BLOCK), 0)
    ki = lax.broadcasted_iota(jnp.int32, (BAND_BLOCK, 2 * BAND_BLOCK), 1)
    dist = BAND_BLOCK + qi - ki
    in_band = (dist >= 0) & (dist <= SPAN)
    first_mask = in_band & (ki >= jnp.where(b > 0, 0, BAND_BLOCK))

    def rows(ref, start):
        if dil == 1:
            return ref[pl.ds(start, BAND_BLOCK), :]
        return ref[pl.ds(start, BAND_BLOCK, stride=dil), :]

    n_hl = LANES // HEAD_DIM
    head_lanes = [slice(hl * HEAD_DIM, (hl + 1) * HEAD_DIM) for hl in range(n_hl)]

    def residues(rs):
        tiles = []
        for r in rs:
            for bi in range(nbb):
                base = bi * BAND_BLOCK * dil + r
                q = rows(q_ref, base).astype(BF16)
                if bi == 0:
                    kp, vp = rows(kp_ref, r), rows(vp_ref, r)
                else:
                    kp, vp = rows(kc_ref, base - BAND_BLOCK * dil), rows(vc_ref, base - BAND_BLOCK * dil)
                k = jnp.concatenate([kp, rows(kc_ref, base)], axis=0).astype(BF16)
                v = jnp.concatenate([vp, rows(vc_ref, base)], axis=0).astype(BF16)
                tiles.append((base, first_mask if bi == 0 else in_band, q, k, v))
        scores = [[lax.dot_general(q[:, sl], k[:, sl], (((1,), (1,)), ((), ())), preferred_element_type=F32)
                   for sl in head_lanes] for _, _, q, k, _ in tiles]
        stats = []
        for (_, mask, _, _, _), ss in zip(tiles, scores):
            per_head = []
            for s in ss:
                s = jnp.where(mask, s, NEG_INF)
                mx = jnp.max(s, axis=-1, keepdims=True)
                p = jnp.exp(s - mx)
                per_head.append((p.astype(BF16), jnp.sum(p, axis=-1, keepdims=True), mx))
            stats.append(per_head)
        for (base, _, _, _, v), per_head in zip(tiles, stats):
            outs = [jnp.dot(p, v[:, sl], preferred_element_type=F32) / den
                    for (p, den, _), sl in zip(per_head, head_lanes)]
            lses = [jnp.broadcast_to(mx + jnp.log(den), (BAND_BLOCK, HEAD_DIM)) for _, den, mx in per_head]
            o = jnp.concatenate(outs, axis=1)
            l = jnp.concatenate(lses, axis=1)
            if dil == 1:
                o_ref[pl.ds(base, BAND_BLOCK), :] = o
                l_ref[pl.ds(base, BAND_BLOCK), :] = l
            else:
                o_ref[pl.ds(base, BAND_BLOCK, stride=dil), :] = o
                l_ref[pl.ds(base, BAND_BLOCK, stride=dil), :] = l

    if dil <= RESIDUE_UNROLL:
        residues(list(range(dil)))
    else:
        def residue_group(rg, carry):
            residues([rg * RESIDUE_UNROLL + ru for ru in range(RESIDUE_UNROLL)])
            return carry

        lax.fori_loop(0, dil // RESIDUE_UNROLL, residue_group, 0)


def _prompt_attention(qk, v, n, s, gi, dil):
    band_rows = BAND_BLOCK * dil
    nbb = max(512 // band_rows, 1)
    rows = band_rows * nbb
    qk4 = qk.reshape(qk.shape[0], n, s, LANES)
    v4 = v.reshape(v.shape[0], n, s, LANES)
    cur = (None, None, rows, LANES)
    prv = (None, None, band_rows, LANES)
    q0 = gi * GROUP_SLABS
    k0 = (N_GROUPS + gi) * GROUP_SLABS
    v0 = gi * GROUP_SLABS

    def prev(b):
        return jnp.maximum(b * nbb - 1, 0)

    in_specs = [
        pl.BlockSpec(cur, lambda a, c, b: (q0 + c, a, b, 0)),
        pl.BlockSpec(prv, lambda a, c, b: (k0 + c, a, prev(b), 0)),
        pl.BlockSpec(cur, lambda a, c, b: (k0 + c, a, b, 0)),
        pl.BlockSpec(prv, lambda a, c, b: (v0 + c, a, prev(b), 0)),
        pl.BlockSpec(cur, lambda a, c, b: (v0 + c, a, b, 0)),
    ]
    o_spec = pl.BlockSpec(cur, lambda a, c, b: (c, a, b, 0))
    out, lse = pl.pallas_call(
        functools.partial(_band_attn_kernel, dil=dil, nbb=nbb), grid=(n, GROUP_SLABS, s // rows),
        in_specs=in_specs, out_specs=[o_spec, o_spec],
        out_shape=[jax.ShapeDtypeStruct((GROUP_SLABS, n, s, LANES), F32)] * 2,
        compiler_params=_cparams(("parallel", "parallel", "arbitrary")), name=f"band_attn_d{dil}",
    )(qk4, qk4, qk4, v4, v4)
    return out.reshape(GROUP_SLABS, n * s, LANES), lse.reshape(GROUP_SLABS, n * s, LANES)


def _cached_attn_kernel(q_ref, c_ref, n_ref, bias_ref, co_ref, o_ref, l_ref, *, win, bb, n_new):
    bias = bias_ref[...]
    lane = lax.broadcasted_iota(jnp.int32, (HEAD_DIM, LANES), 1)
    is_new = lane >= LANES - n_new

    def body(b, carry):
        for h in range(N_SLOTS):
            hs, hl = h // 2, h % 2
            sl = slice(hl * HEAD_DIM, (hl + 1) * HEAD_DIM)
            q = q_ref[hs, b][:, sl]
            kt = jnp.concatenate([c_ref[b, 0, h], n_ref[b, 0, h]], axis=1)
            vt = jnp.concatenate([c_ref[b, 1, h], n_ref[b, 1, h]], axis=1)
            s = jnp.dot(q, kt, preferred_element_type=F32) + bias
            mx = jnp.max(s, axis=-1, keepdims=True)
            p = jnp.exp(s - mx)
            den = jnp.sum(p, axis=-1, keepdims=True)
            o = lax.dot_general(p, vt, (((1,), (1,)), ((), ())), preferred_element_type=F32) / den
            o_ref[hs, b, :, sl] = o
            l_ref[hs, b, :, sl] = jnp.broadcast_to(mx + jnp.log(den), (n_new, HEAD_DIM))
            for kv in range(2):
                rolled = pltpu.roll(c_ref[b, kv, h], win - n_new, 1)
                if win > LANES:
                    co_ref[b, kv, h, :, :win - LANES] = rolled[:, :win - LANES]
                co_ref[b, kv, h, :, win - LANES:] = jnp.where(is_new, n_ref[b, kv, h], rolled[:, win - LANES:])
        return carry

    lax.fori_loop(0, bb, body, 0)


def _cache_bias(win, dil, n_new):
    t = np.arange(n_new)[:, None]
    j = np.arange(win)[None, :]
    dist = win + t - j
    ok_c = (dist % dil == 0) & (dist <= dil * SPAN)
    l = np.arange(LANES)[None, :]
    tn = l - (LANES - n_new)
    ok_n = (tn >= 0) & (tn <= t) & ((t - tn) % dil == 0)
    ok = np.concatenate([ok_c, ok_n], axis=1)
    return jnp.asarray(np.where(ok, 0.0, NEG_INF), dtype=F32)


def _sample_attention(qk, new_t, cache, nb_, n_new, gi, win, dil):
    bb = max(1, min(8, 2048 // win))
    while nb_ % bb:
        bb //= 2
    qk4 = qk.reshape(qk.shape[0], nb_, n_new, LANES)
    cache_t = jnp.transpose(cache, (0, 2, 3, 4, 1))
    bias = _cache_bias(win, dil, n_new)
    c_blk = (bb, 2, N_SLOTS, HEAD_DIM, win)
    n_blk = (bb, 2, N_SLOTS, HEAD_DIM, LANES)
    q_blk = (GROUP_SLABS, bb, n_new, LANES)
    in_specs = [
        pl.BlockSpec(q_blk, lambda i: (gi, i, 0, 0)),
        pl.BlockSpec(c_blk, lambda i: (i, 0, 0, 0, 0)),
        pl.BlockSpec(n_blk, lambda i: (i, 0, 0, 0, 0)),
        pl.BlockSpec(bias.shape, lambda i: (0, 0)),
    ]
    o_spec = pl.BlockSpec(q_blk, lambda i: (0, i, 0, 0))
    cache_o, out, lse = pl.pallas_call(
        functools.partial(_cached_attn_kernel, win=win, bb=bb, n_new=n_new),
        grid=(nb_ // bb,),
        in_specs=in_specs,
        out_specs=[pl.BlockSpec(c_blk, lambda i: (i, 0, 0, 0, 0)), o_spec, o_spec],
        out_shape=[jax.ShapeDtypeStruct(cache_t.shape, F32),
                   jax.ShapeDtypeStruct((GROUP_SLABS, nb_, n_new, LANES), F32),
                   jax.ShapeDtypeStruct((GROUP_SLABS, nb_, n_new, LANES), F32)],
        compiler_params=_cparams(("parallel",)), name=f"cached_attn_d{dil}",
    )(qk4, cache_t, new_t, bias)
    new_cache = jnp.transpose(cache_o, (0, 4, 1, 2, 3))
    return (out.reshape(GROUP_SLABS, nb_ * n_new, LANES), lse.reshape(GROUP_SLABS, nb_ * n_new, LANES), new_cache)


def _layer_norm(x, g, b):
    mu = jnp.mean(x, axis=-1, keepdims=True)
    xc = x - mu
    var = jnp.mean(xc * xc, axis=-1, keepdims=True)
    return xc * lax.rsqrt(var + LN_EPS) * g + b


def _mix_kernel(x_ref, o0_ref, o1_ref, o2_ref, l0_ref, l1_ref, l2_ref, u_ref, h1_ref, h2_ref,
                cb_ref, ga_ref, gc_ref, cw_ref, wpa_ref, wpc_ref, wo_ref, g_ref, b_ref, out_ref,
                *, seg_len, alpha):
    tm = x_ref.shape[0]
    parts = []
    for c in range(GROUP_SLABS):
        l0, l1, l2 = l0_ref[c], l1_ref[c], l2_ref[c]
        lm = jnp.maximum(jnp.maximum(l0, l1), l2)
        w0, w1, w2 = jnp.exp(l0 - lm), jnp.exp(l1 - lm), jnp.exp(l2 - lm)
        parts.append((w0 * o0_ref[c] + w1 * o1_ref[c] + w2 * o2_ref[c]) / (w0 + w1 + w2))
    y_att = jnp.concatenate(parts, axis=1)
    u = u_ref[...]
    row = lax.broadcasted_iota(jnp.int32, u.shape, 0)
    if seg_len >= tm:
        keep = jnp.where((pl.program_id(0) * tm) % seg_len == 0, 0.0, 1.0).astype(F32)
        hrow1 = h1_ref[7:8, :] * keep
        hrow2 = h1_ref[6:7, :] * keep
        prev1 = jnp.where(row == 0, hrow1, pltpu.roll(u, 1, 0))
        prev2 = jnp.where(row == 0, hrow2, jnp.where(row == 1, hrow1, pltpu.roll(u, 2, 0)))
    else:
        rs = row & (seg_len - 1)
        prev1 = jnp.where(rs == 0, h1_ref[...], pltpu.roll(u, 1, 0))
        prev2 = jnp.where(rs < 2, h2_ref[...], pltpu.roll(u, 2, 0))
    cw = cw_ref[...]
    z_conv = prev2 * cw[0:1, :] + prev1 * cw[1:2, :] + u * cw[2:3, :]
    cv = cb_ref[...] * z_conv
    br_a = jnp.dot(y_att.astype(BF16), wpa_ref[...], preferred_element_type=F32)
    br_c = jnp.dot(cv.astype(BF16), wpc_ref[...], preferred_element_type=F32)
    merged = jax.nn.sigmoid(ga_ref[...]) * br_a + jax.nn.sigmoid(gc_ref[...]) * br_c
    mix = jnp.dot(merged.astype(BF16), wo_ref[...], preferred_element_type=F32)
    out_ref[...] = _layer_norm(alpha * x_ref[...] + mix, g_ref[...], b_ref[...])


def _mixer_finish(x, outs, lses, u, rest, hist1, hist2, seg_len, conv_w, wpa, wpc, wo, ln_g, ln_b, alpha):
    t = x.shape[0]
    tm = min(512, t)
    nt = t // tm
    row_spec = pl.BlockSpec((tm, D_MODEL), lambda i: (i, 0))
    att_spec = pl.BlockSpec((GROUP_SLABS, tm, LANES), lambda i: (0, i, 0))
    if seg_len >= tm:
        hb = tm // 8
        h1_spec = pl.BlockSpec((8, D_MODEL), lambda i: (jnp.maximum(i * hb - 1, 0), 0))
        h2_spec = h1_spec
    else:
        h1_spec = h2_spec = row_spec

    def full(shape):
        return pl.BlockSpec(shape, lambda i: (0,) * len(shape))

    in_specs = ([row_spec] + [att_spec] * 6 + [row_spec, h1_spec, h2_spec]
                + [pl.BlockSpec((tm, D_MODEL), lambda i, c=c: (i, c)) for c in range(3)]
                + [full((CONV_WIDTH, D_MODEL)), full((GROUP_W, D_MODEL)), full((D_MODEL, D_MODEL)),
                   full((D_MODEL, D_MODEL)), full((1, D_MODEL)), full((1, D_MODEL))])
    return pl.pallas_call(
        functools.partial(_mix_kernel, seg_len=seg_len, alpha=alpha),
        grid=(nt,), in_specs=in_specs, out_specs=row_spec,
        out_shape=jax.ShapeDtypeStruct((t, D_MODEL), F32),
        compiler_params=_cparams(("parallel",)), name="mixer_finish",
    )(x, *outs, *lses, u, hist1, hist2, rest, rest, rest, conv_w, wpa, wpc, wo, ln_g, ln_b)


def _ranked_top16(s, exact_ties):
    n_keys = s.shape[0]
    rowf = lax.broadcasted_iota(jnp.int32, s.shape, 0).astype(F32)
    rank = jnp.full(s.shape, float(PEER_N_KEYS - 1), F32)
    tops = []
    for k in range(PEER_TOPK):
        m = jnp.max(s, axis=0, keepdims=True)
        if exact_ties:
            pos = jnp.min(jnp.where(s == m, rowf, float(n_keys)), axis=0, keepdims=True)
            hit = rowf == pos
        else:
            hit = s == m
        s = jnp.where(hit, -jnp.inf, s)
        rank = jnp.where(hit, float(k), rank)
        tops.append(m)
    return rank, tops


def _not_16(flags):
    total = jnp.sum(flags, axis=0, keepdims=True)
    return jnp.where(total == float(PEER_TOPK), 0.0, 1.0)


def _rows_to_tile(rows):
    tmw = rows[0].shape[1]
    ri = lax.broadcasted_iota(jnp.int32, (8, tmw), 0)
    out = jnp.zeros((8, tmw), F32)
    for a, rw in enumerate(rows):
        out = jnp.where(ri == a, rw, out)
    return out


def _select_counts(t1, t2, exact_ties):
    tmw = t1[0].shape[1]
    ri = lax.broadcasted_iota(jnp.int32, (8, tmw), 0)
    rif = ri.astype(F32)
    ninf = -jnp.inf
    v1lo, v1hi = _rows_to_tile(t1[:8]), _rows_to_tile(t1[8:])
    v2lo, v2hi = _rows_to_tile(t2[:8]), _rows_to_tile(t2[8:])
    all_rows = ri >= 0
    tiles = [
        (t1[0] + v2lo, rif, all_rows),
        (t1[0] + v2hi, 8.0 + rif, all_rows),
        (t1[1] + v2lo, 16.0 + rif, all_rows),
        (v1hi + t2[0], (8.0 + rif) * 16.0, all_rows),
        (v1lo + t2[0], rif * 16.0, ri >= 2),
        (v1lo + t2[1], rif * 16.0 + 1.0, ri >= 2),
        (t1[2] + v2lo, 32.0 + rif, (ri >= 2) & (ri <= 4)),
        (t1[3] + v2lo, 48.0 + rif, (ri >= 2) & (ri <= 3)),
        (t1[4] + v2lo, 64.0 + rif, ri == 2),
    ]
    vals = [jnp.where(ok, sm, ninf) for sm, _, ok in tiles]
    poss = [jnp.where(ok, ps, -1.0) for _, ps, ok in tiles]
    hits = [jnp.zeros((8, tmw), F32) for _ in vals]
    top = t1[0] + t2[0]
    z = jnp.zeros((1, tmw), F32)
    big = 1e9
    for _ in range(PEER_TOPK):
        m8 = vals[0]
        for vv in vals[1:]:
            m8 = jnp.maximum(m8, vv)
        m = jnp.max(m8, axis=0, keepdims=True)
        if exact_ties:
            p8 = jnp.where(vals[0] == m, poss[0], big)
            for vv, pp in zip(vals[1:], poss[1:]):
                p8 = jnp.minimum(p8, jnp.where(vv == m, pp, big))
            pm = jnp.min(p8, axis=0, keepdims=True)
        for c in range(len(vals)):
            hit = (poss[c] == pm) if exact_ties else (vals[c] == m)
            vals[c] = jnp.where(hit, ninf, vals[c])
            hits[c] = jnp.where(hit, 1.0, hits[c])
        z = z + jnp.exp(m - top)

    def colsum(x):
        return jnp.sum(x, axis=0, keepdims=True)

    n_picked = hits[0]
    for hh in hits[1:]:
        n_picked = n_picked + hh
    not16 = _not_16(n_picked)
    lo = hits[4] + hits[5]
    counts = [colsum(hits[0]) + colsum(hits[1]), colsum(hits[2]),
              lo[2:3] + colsum(hits[6]), lo[3:4] + colsum(hits[7]), lo[4:5] + colsum(hits[8]),
              lo[5:6], lo[6:7], lo[7:8]]
    counts += [hits[3][a:a + 1] for a in range(8)]
    return counts, z, not16


def _packed_rows(row, n_rows):
    tile = jnp.broadcast_to(row, (16, row.shape[1])).astype(BF16)
    return jnp.concatenate([tile] * (n_rows // 16), axis=0)


def _peer_kernel(hn_ref, hp_ref, wq_ref, k1_ref, k2_ref, u_ref, vt_ref, g_ref, b_ref, out_ref,
                 ht_ref, c1_ref, e1_ref, r2_ref, e2_ref, a_ref, wt_ref, ft_ref, *, alpha, eb, n_chunks,
                 parts_per_step):
    i = pl.program_id(0)
    e = pl.program_id(1)
    n_e = pl.num_programs(1)
    half = PEER_D_KEY // 2
    rows_per_step = eb // PEER_N_KEYS
    cw = ht_ref.shape[-1]
    sel_slot = i % 2
    use_slot = (i + 1) % 2

    @pl.when((i == 0) & (e == 0))
    def _init():
        ht_ref[1] = jnp.zeros(ht_ref.shape[1:], BF16)
        c1_ref[1] = jnp.zeros(c1_ref.shape[1:], F32)
        e1_ref[1] = jnp.zeros(e1_ref.shape[1:], F32)
        r2_ref[1] = jnp.zeros(r2_ref.shape[1:], BF16)
        e2_ref[1] = jnp.zeros(e2_ref.shape[1:], BF16)

    @pl.when(e == 0)
    def _start_tile():
        ht = hn_ref[...].T.astype(BF16)
        for c in range(n_chunks):
            ht_ref[sel_slot, c] = ht[:, c * cw:(c + 1) * cw]
        ft_ref[...] = jnp.zeros(ft_ref.shape, F32)

    n_parts = PEER_HEADS * n_chunks

    def select_part(k, exact_ties):
        part = (e * parts_per_step + k) % n_parts
        hh = part // n_chunks
        sc = part % n_chunks
        off = pl.multiple_of(hh * PEER_D_KEY, PEER_D_KEY)
        qy = jnp.dot(wq_ref[pl.ds(off, PEER_D_KEY), :], ht_ref[sel_slot, sc], preferred_element_type=F32)
        s1 = jnp.dot(k1_ref[...], qy[:half].astype(BF16), preferred_element_type=F32)
        s2 = jnp.dot(k2_ref[...], qy[half:].astype(BF16), preferred_element_type=F32)
        suspect = jnp.zeros((1, LANES), F32)
        for l0 in range(0, cw, LANES):
            ls = slice(l0, l0 + LANES)
            s1c, s2c = s1[:, ls], s2[:, ls]
            e1_ref[sel_slot, sc, hh, :, ls] = jnp.exp(s1c - jnp.max(s1c, axis=0, keepdims=True))
            e2_ref[sel_slot, sc, hh, :, ls] = jnp.exp(s2c - jnp.max(s2c, axis=0, keepdims=True)).astype(BF16)
            r2, t2 = _ranked_top16(s2c, exact_ties)
            r2_ref[sel_slot, sc, hh, :, ls] = r2.astype(BF16)
            r1, t1 = _ranked_top16(s1c, exact_ties)
            counts, z, not16 = _select_counts(t1, t2, exact_ties)
            c1 = jnp.zeros(s1c.shape, F32)
            for a in range(PEER_TOPK):
                c1 = jnp.where(r1 == float(a), counts[a], c1)
            c1_ref[sel_slot, sc, hh, :, ls] = c1
            e1_ref[sel_slot, sc, hh, :, ls] = e1_ref[sel_slot, sc, hh, :, ls] / z
            if not exact_ties:
                ranked = float(PEER_N_KEYS - 1)
                suspect = jnp.maximum(suspect, not16)
                suspect = jnp.maximum(suspect, _not_16(jnp.where(r1 < ranked, 1.0, 0.0)))
                suspect = jnp.maximum(suspect, _not_16(jnp.where(r2 < ranked, 1.0, 0.0)))
        return suspect

    suspect = select_part(0, exact_ties=False)
    for k in range(1, parts_per_step):
        suspect = jnp.maximum(suspect, select_part(k, exact_ties=False))
    any_suspect = jnp.max(suspect)

    for c in range(n_chunks):
        a_ref[c] = jnp.dot(u_ref[...], ht_ref[use_slot, c], preferred_element_type=F32)
    for c in range(n_chunks):
        for il in range(rows_per_step):
            row = e * rows_per_step + il
            rs = slice(il * PEER_N_KEYS, (il + 1) * PEER_N_KEYS)
            pre = a_ref[c, rs, :]
            act = 0.5 * pre * (1.0 + lax.erf(pre * math.sqrt(0.5)))
            gate = jnp.zeros(pre.shape, BF16)
            for h2 in range(PEER_HEADS):
                cnt = _packed_rows(c1_ref[use_slot, c, h2, pl.ds(row, 1), :], PEER_N_KEYS)
                wgt = _packed_rows(e1_ref[use_slot, c, h2, pl.ds(row, 1), :], PEER_N_KEYS)
                gate = gate + jnp.where(r2_ref[use_slot, c, h2] < cnt, e2_ref[use_slot, c, h2],
                                        jnp.zeros((), BF16)) * wgt
            wt_ref[c, rs, :] = gate * act.astype(BF16)
        ft_ref[c] += jnp.dot(vt_ref[...], wt_ref[c], preferred_element_type=F32)

    @pl.when(any_suspect > 0.0)
    def _redo_with_ties():
        for k in range(parts_per_step):
            select_part(k, exact_ties=True)

    @pl.when((e == n_e - 1) & (i > 0))
    def _finish():
        f = jnp.concatenate([ft_ref[c].T for c in range(n_chunks)], axis=0)
        out_ref[...] = _layer_norm(alpha * hp_ref[...] + f, g_ref[...], b_ref[...])


def _peer(h, wq_t, k1, k2, u_bf, vt_bf, ln_g, ln_b, alpha):
    t = h.shape[0]
    tm = min(512, t)
    eb = PEER_EXPERT_BLOCK
    n_exp = u_bf.shape[0]
    nt = t // tm
    n_e = n_exp // eb
    cw = min(TOKEN_CHUNK, tm)
    n_chunks = tm // cw
    n_parts = PEER_HEADS * n_chunks
    parts_per_step = -(-n_parts // n_e)
    assert (n_e * parts_per_step) % n_parts == 0

    def full(shape):
        return pl.BlockSpec(shape, lambda i, e: (0,) * len(shape))

    next_spec = pl.BlockSpec((tm, D_MODEL), lambda i, e: (jnp.minimum(i, nt - 1), 0))
    prev_spec = pl.BlockSpec((tm, D_MODEL), lambda i, e: (jnp.maximum(i - 1, 0), 0))
    meta32 = pltpu.VMEM((2, n_chunks, PEER_HEADS, PEER_N_KEYS, cw), F32)
    meta16 = pltpu.VMEM((2, n_chunks, PEER_HEADS, PEER_N_KEYS, cw), BF16)
    return pl.pallas_call(
        functools.partial(_peer_kernel, alpha=alpha, eb=eb, n_chunks=n_chunks, parts_per_step=parts_per_step),
        grid=(nt + 1, n_e),
        in_specs=[next_spec, prev_spec, full(wq_t.shape), full(k1.shape), full(k2.shape),
                  pl.BlockSpec((eb, D_MODEL), lambda i, e: (e, 0)),
                  pl.BlockSpec((D_MODEL, eb), lambda i, e: (0, e)),
                  full((1, D_MODEL)), full((1, D_MODEL))],
        out_specs=prev_spec,
        out_shape=jax.ShapeDtypeStruct((t, D_MODEL), F32),
        scratch_shapes=[pltpu.VMEM((2, n_chunks, D_MODEL, cw), BF16), meta32, meta32, meta16, meta16,
                        pltpu.VMEM((n_chunks, eb, cw), F32), pltpu.VMEM((n_chunks, eb, cw), BF16),
                        pltpu.VMEM((n_chunks, D_MODEL, cw), F32)],
        compiler_params=_cparams(("arbitrary", "arbitrary")), name="peer",
    )(h, h, wq_t, k1, k2, u_bf, vt_bf, ln_g, ln_b)


def _rope_tables(pos):
    half = HEAD_DIM // 2
    inv_freq = ROPE_THETA ** (-jnp.arange(half, dtype=F32) / half)
    ang = pos.astype(F32)[:, None] * inv_freq[None, :]
    cos, sin = jnp.cos(ang), jnp.sin(ang)
    cos_t = jnp.concatenate([cos, cos, cos, cos], axis=-1)
    sin_t = jnp.concatenate([-sin, sin, -sin, sin], axis=-1)
    return cos_t, sin_t


def _slab_rows_to_heads(slabs, lead):
    nl = len(lead)
    x = jnp.moveaxis(slabs, 0, nl + 1)
    return x.reshape(*lead, x.shape[nl], N_SLOTS, HEAD_DIM)


def kernel(x_prompt, x_sample, cache_kv_w128, cache_kv_w512, cache_kv_w2048, state_conv, w_in, conv_w, w_branch_attn, w_branch_conv, w_out, ln1_g, ln1_b, peer_w_q, peer_keys_1, peer_keys_2, peer_u, peer_v, ln2_g, ln2_b):
    depth = w_in.shape[0]
    assert depth == 1, "single-layer trunk"
    n, s, d = x_prompt.shape
    nb_, n_new, _ = x_sample.shape
    past_len = 8192
    alpha = (2.0 * depth) ** 0.25
    caches = (cache_kv_w128, cache_kv_w512, cache_kv_w2048)
    for (win, dil), c in zip(DIL_GROUPS, caches):
        assert c.shape[2] == win and s % (dil * BAND_BLOCK) == 0 and win // dil == SPAN and win <= s

    lyr = 0
    w_bf = w_in[lyr].astype(BF16)
    wpa = w_branch_attn[lyr].astype(BF16)
    wpc = w_branch_conv[lyr].astype(BF16)
    wo = w_out[lyr].astype(BF16)
    wq_t = peer_w_q[lyr].T.astype(BF16)
    k1 = peer_keys_1[lyr].astype(BF16)
    k2 = peer_keys_2[lyr].astype(BF16)
    u_bf = peer_u[lyr].astype(BF16)
    vt_bf = peer_v[lyr].T.astype(BF16)
    g1, b1 = ln1_g[lyr][None, :], ln1_b[lyr][None, :]
    g2, b2 = ln2_g[lyr][None, :], ln2_b[lyr][None, :]
    cw = conv_w[lyr]
    k_slab0 = N_GROUPS * GROUP_SLABS

    xp = x_prompt.reshape(n * s, d)
    cos_p, sin_p = _rope_tables(jnp.arange(s))
    qk_p, v_p, u_p, rest_p = _input_projections(xp, w_bf, cos_p, sin_p)
    outs, lses = [], []
    for gi, (win, dil) in enumerate(DIL_GROUPS):
        o, l = _prompt_attention(qk_p, v_p, n, s, gi, dil)
        outs.append(o)
        lses.append(l)
    h_p = _mixer_finish(xp, outs, lses, u_p, rest_p, u_p, u_p, s, cw, wpa, wpc, wo, g1, b1, alpha)
    y_p = _peer(h_p, wq_t, k1, k2, u_bf, vt_bf, g2, b2, alpha).reshape(n, s, d)

    qk_p4 = qk_p.reshape(qk_p.shape[0], n, s, LANES)
    v_p4 = v_p.reshape(v_p.shape[0], n, s, LANES)
    kv_p = []
    for gi, (win, dil) in enumerate(DIL_GROUPS):
        ks = k_slab0 + gi * GROUP_SLABS
        vs = gi * GROUP_SLABS
        k_tail = _slab_rows_to_heads(qk_p4[ks:ks + GROUP_SLABS, :, s - win:], (n,))
        v_tail = _slab_rows_to_heads(v_p4[vs:vs + GROUP_SLABS, :, s - win:], (n,))
        kv_p.append(jnp.stack([k_tail, v_tail], axis=2)[None])
    conv_state_p = u_p.reshape(n, s, d)[:, s - (CONV_WIDTH - 1):][None]

    xs = x_sample.reshape(nb_ * n_new, d)
    cos_s, sin_s = _rope_tables(jnp.tile(past_len + jnp.arange(n_new), nb_))
    qk_s, v_s, u_s, rest_s = _input_projections(xs, w_bf, cos_s, sin_s)
    qk_s4 = qk_s.reshape(qk_s.shape[0], nb_, n_new, LANES)
    v_s4 = v_s.reshape(v_s.shape[0], nb_, n_new, LANES)
    outs, lses, kv_s = [], [], []
    for gi, (win, dil) in enumerate(DIL_GROUPS):
        ks = k_slab0 + gi * GROUP_SLABS
        vs = gi * GROUP_SLABS
        k_new = _slab_rows_to_heads(qk_s4[ks:ks + GROUP_SLABS], (nb_,))
        v_new = _slab_rows_to_heads(v_s4[vs:vs + GROUP_SLABS], (nb_,))
        new_t = jnp.transpose(jnp.stack([k_new, v_new], axis=1), (0, 1, 3, 4, 2))
        new_t = jnp.pad(new_t, ((0, 0),) * 4 + ((LANES - n_new, 0),))
        o, l, cache_new = _sample_attention(qk_s, new_t, caches[gi][lyr], nb_, n_new, gi, win, dil)
        outs.append(o)
        lses.append(l)
        kv_s.append(cache_new[None])
    st = state_conv[lyr]
    zeros6 = jnp.zeros((nb_, n_new - 1, d), F32)
    hist1 = jnp.concatenate([st[:, 1:2], zeros6], axis=1).reshape(nb_ * n_new, d)
    hist2 = jnp.concatenate([st, zeros6[:, 1:]], axis=1).reshape(nb_ * n_new, d)
    h_s = _mixer_finish(xs, outs, lses, u_s, rest_s, hist1, hist2, n_new, cw, wpa, wpc, wo, g1, b1, alpha)
    y_s = _peer(h_s, wq_t, k1, k2, u_bf, vt_bf, g2, b2, alpha).reshape(nb_, n_new, d)

    u_ext_s = jnp.concatenate([st, u_s.reshape(nb_, n_new, d)], axis=1)
    conv_state_s = u_ext_s[:, u_ext_s.shape[1] - (CONV_WIDTH - 1):][None]

    return (y_p, y_s, kv_p[0], kv_p[1], kv_p[2], conv_state_p,
            kv_s[0], kv_s[1], kv_s[2], conv_state_s)
```
